```python
import jax, jax.numpy as jnp
from jax import lax
import numpy as np

D_MODEL = 1024
BATCH = 32
SEQ = 256
DEPTH = 1
DEC_BATCH = 4
DEC_SEQ = 1024
PAST_LEN = 256

GRID_W = 64
HEAD_DIM = 64
A_WIDTH = D_MODEL
A_HEADS = A_WIDTH // HEAD_DIM
DECAY_LORA = 64
ICLR_LORA = 64
CONV_W = 3
B_WIDTH = D_MODEL
B_Q_HEADS = B_WIDTH // HEAD_DIM
B_KV_HEADS = 4
B_GROUPS = B_Q_HEADS // B_KV_HEADS
KV_WIDTH = B_KV_HEADS * HEAD_DIM
WINDOW = 128
BLOCK = 128
KSPAN = BLOCK + 2 * WINDOW
ROPE_BASE = 10000.0
RMS_EPS = 1e-6
GN_EPS = 64e-5
NEG_INF = -1e30
ATTN_SCALE = HEAD_DIM ** -0.5
A_COLS = 3 * A_WIDTH + DECAY_LORA + ICLR_LORA
IN_SIZES = (A_COLS, A_WIDTH, B_WIDTH, KV_WIDTH, KV_WIDTH, B_WIDTH, D_MODEL, D_MODEL)
IN_WIDTH = A_COLS + A_WIDTH + 2 * B_WIDTH + 2 * KV_WIDTH + 2 * D_MODEL

kernel_name = "hybrid_rwkv7_swa_prefix_dit_step"

F32 = jnp.float32


def _split_points(sizes):
    pts, acc = [], 0
    for s in sizes[:-1]:
        acc += s
        pts.append(acc)
    return pts


def _rmsnorm(x, w):
    xf = x.astype(F32)
    y = xf * lax.rsqrt(jnp.mean(xf * xf, axis=-1, keepdims=True) + RMS_EPS)
    return (y * w).astype(x.dtype)


def _modulation(cond, w_ada, b_ada):
    m = jax.nn.silu(cond) @ w_ada + b_ada
    shift, scale, gate = jnp.split(m, 3, axis=-1)
    return shift[:, None], scale[:, None], gate[:, None]


def _short_conv(z, w):
    zp = jnp.pad(z, ((0, 0), (1, 1), (0, 0)))
    return w[0] * zp[:, :-2] + w[1] * z + w[2] * zp[:, 2:]


def _axial_rope(x):
    T = x.shape[-2]
    n_rows = T // GRID_W
    row = jnp.repeat(jnp.arange(n_rows), GRID_W)
    col = jnp.tile(jnp.arange(GRID_W), n_rows)
    half = HEAD_DIM // 2
    nf = half // 2
    inv = ROPE_BASE ** (-jnp.arange(nf, dtype=F32) / nf)

    def rot(xp, pos):
        ang = pos.astype(F32)[:, None] * inv[None, :]
        cos = jnp.cos(ang).astype(x.dtype)
        sin = jnp.sin(ang).astype(x.dtype)
        x1, x2 = xp[..., :nf], xp[..., nf:]
        return jnp.concatenate([x1 * cos - x2 * sin, x1 * sin + x2 * cos], axis=-1)

    return jnp.concatenate([rot(x[..., :half], row), rot(x[..., half:], col)], axis=-1)


def _wkv_scan(s0, r, w, k, v, a, b, reverse):
    def tm(t):
        return jnp.swapaxes(t.astype(F32), 0, 1)

    def step(s, inp):
        r_t, w_t, k_t, v_t, a_t, b_t = inp
        sa = jnp.einsum('bhvk,bhk->bhv', s, a_t)
        s = s * w_t[:, :, None, :] + sa[..., None] * b_t[:, :, None, :] + v_t[..., None] * k_t[:, :, None, :]
        return s, jnp.einsum('bhvk,bhk->bhv', s, r_t)

    s_fin, y = lax.scan(step, s0.astype(F32), (tm(r), tm(w), tm(k), tm(v), tm(a), tm(b)), reverse=reverse)
    return s_fin, jnp.swapaxes(y, 0, 1)


def _rwkv_branch(za, s0, p):
    B, T, _ = za.shape
    r, k, v, wd, ad = jnp.split(za, [A_WIDTH, 2 * A_WIDTH, 3 * A_WIDTH, 3 * A_WIDTH + DECAY_LORA], axis=-1)

    def hd(t):
        return t.reshape(B, T, A_HEADS, HEAD_DIM)

    kk = hd((k * p['k_k']).astype(F32))
    kk = kk * lax.rsqrt(jnp.maximum(jnp.sum(kk * kk, axis=-1, keepdims=True), 1e-12))
    wd_t = jnp.tanh(wd)

    def direction(d):
        w_log = -jax.nn.softplus(-(p['w0'][d] + wd_t @ p['w_up'][d]).astype(F32)) - 0.5
        decay = jnp.exp(-jnp.exp(w_log))
        a = jax.nn.sigmoid((p['a0'][d] + ad @ p['a_up'][d]).astype(F32))
        k_d = k * (1.0 + (a - 1.0) * p['k_a'])
        return _wkv_scan(s0[:, d], hd(r), hd(decay), hd(k_d), hd(v), -kk, kk * hd(a), reverse=(d == 1))

    s_fwd, y_fwd = direction(0)
    s_bwd, y_bwd = direction(1)
    y = y_fwd + y_bwd
    mu = jnp.mean(y, axis=-1, keepdims=True)
    var = jnp.mean(jnp.square(y - mu), axis=-1, keepdims=True)
    y = ((y - mu) * lax.rsqrt(var + GN_EPS)).reshape(B, T, A_WIDTH) * p['ln_x_w'] + p['ln_x_b']
    bonus = jnp.sum(hd(r) * hd(k) * p['r_k'], axis=-1, keepdims=True) * hd(v)
    y = y + bonus.reshape(B, T, A_WIDTH)
    states = jnp.stack([s_fwd, s_bwd], axis=1)
    return y.astype(za.dtype), states.astype(za.dtype)


def _heads_q(q):
    B, T, _ = q.shape
    return q.reshape(B, T, B_KV_HEADS, B_GROUPS, HEAD_DIM).transpose(0, 2, 3, 1, 4)


def _heads_kv(t):
    B, T, _ = t.shape
    return t.reshape(B, T, B_KV_HEADS, HEAD_DIM).transpose(0, 2, 1, 3)


def _merge_heads(o):
    B, _, _, T, _ = o.shape
    return o.transpose(0, 3, 1, 2, 4).reshape(B, T, B_WIDTH)


def _sink_softmax(s, sink):
    col = jnp.broadcast_to(sink.astype(F32)[None, :, :, None, None], s.shape[:-1] + (1,))
    return jax.nn.softmax(jnp.concatenate([s, col], axis=-1), axis=-1)[..., :-1]


def _context_attention(q, k, v, sink):
    B, Hk, G, T, Dh = q.shape
    nb = T // BLOCK
    qb = jnp.moveaxis(q.reshape(B, Hk, G, nb, BLOCK, Dh), 3, 0)

    def one(qi):
        s = jnp.einsum('bhgqd,bhkd->bhgqk', qi, k, preferred_element_type=F32) * ATTN_SCALE
        pr = _sink_softmax(s, sink).astype(v.dtype)
        return jnp.einsum('bhgqk,bhkd->bhgqd', pr, v)

    o = lax.map(one, qb)
    return jnp.moveaxis(o, 0, 3).reshape(B, Hk, G, T, Dh)


def _latent_attention(q, k, v, k_ctx, v_ctx, sink):
    B, Hk, G, T, Dh = q.shape
    nb = T // BLOCK
    kp = jnp.pad(k, ((0, 0), (0, 0), (WINDOW, WINDOW), (0, 0)))
    vp = jnp.pad(v, ((0, 0), (0, 0), (WINDOW, WINDOW), (0, 0)))
    qrel = jnp.arange(BLOCK)
    krel = jnp.arange(KSPAN) - WINDOW
    band = jnp.abs(qrel[:, None] - krel[None, :]) <= WINDOW

    def one(i):
        start = i * BLOCK
        qi = lax.dynamic_slice_in_dim(q, start, BLOCK, axis=3)
        ki = lax.dynamic_slice_in_dim(kp, start, KSPAN, axis=2)
        vi = lax.dynamic_slice_in_dim(vp, start, KSPAN, axis=2)
        kabs = start + krel
        mask = band & ((kabs >= 0) & (kabs < T))[None, :]
        s_loc = jnp.einsum('bhgqd,bhkd->bhgqk', qi, ki, preferred_element_type=F32) * ATTN_SCALE
        s_loc = jnp.where(mask, s_loc, NEG_INF)
        s_ctx = jnp.einsum('bhgqd,bhkd->bhgqk', qi, k_ctx, preferred_element_type=F32) * ATTN_SCALE
        pr = _sink_softmax(jnp.concatenate([s_loc, s_ctx], axis=-1), sink).astype(v.dtype)
        return (jnp.einsum('bhgqk,bhkd->bhgqd', pr[..., :KSPAN], vi)
                + jnp.einsum('bhgqk,bhkd->bhgqd', pr[..., KSPAN:], v_ctx))

    o = lax.map(one, jnp.arange(nb))
    return jnp.moveaxis(o, 0, 3).reshape(B, Hk, G, T, Dh)


def _mixer_inputs(x, cond, p):
    shift, scale, gate = _modulation(cond, p['w_ada'], p['b_ada'])
    h = _rmsnorm(x, p['norm_w']) * (1.0 + scale) + shift
    z = h @ p['w_in']
    za, ga, q, kb, vb, gb, ma, mb = jnp.split(z, _split_points(IN_SIZES), axis=-1)
    za = _short_conv(za, p['conv_a'])
    return gate, za, ga, q, kb, vb, gb, ma, mb


def _merge(x, gate, y_a, g_a, y_b, g_b, m_a, m_b, p):
    branch_a = (y_a * jax.nn.silu(g_a)) @ p['w_oA']
    branch_b = (y_b * jax.nn.silu(g_b)) @ p['w_oB']
    merged = jax.nn.sigmoid(m_a) * branch_a + jax.nn.sigmoid(m_b) * branch_b
    return x + gate * (merged @ p['w_out'])


def _context_layer(x, c_ctx, p):
    B = x.shape[0]
    gate, za, ga, q, kb, vb, gb, ma, mb = _mixer_inputs(x, c_ctx[None], p)
    s0 = jnp.zeros((B, 2, A_HEADS, HEAD_DIM, HEAD_DIM), za.dtype)
    y_a, s_ctx = _rwkv_branch(za, s0, p)
    k_c, v_c = _heads_kv(kb), _heads_kv(vb)
    y_b = _merge_heads(_context_attention(_heads_q(q), k_c, v_c, p['sink']))
    return _merge(x, gate, y_a, ga, y_b, gb, ma, mb, p), k_c, v_c, s_ctx


def _latent_layer(x, c, k_ctx, v_ctx, s_ctx, p):
    gate, za, ga, q, kb, vb, gb, ma, mb = _mixer_inputs(x, c, p)
    y_a, _ = _rwkv_branch(za, s_ctx, p)
    qh = _axial_rope(_heads_q(q))
    kh = _axial_rope(_heads_kv(kb))
    y_b = _merge_heads(_latent_attention(qh, kh, _heads_kv(vb), k_ctx, v_ctx, p['sink']))
    return _merge(x, gate, y_a, ga, y_b, gb, ma, mb, p)


def setup_inputs(seed: int = 0) -> dict:
    key = jax.random.key(seed)
    ks = jax.random.split(key, 32)

    def nrm(i, shape, s):
        return s * jax.random.normal(ks[i], shape, F32)

    L = DEPTH
    return {
        "x_prompt": nrm(0, (BATCH, SEQ, D_MODEL), 1.0),
        "x_sample": nrm(1, (DEC_BATCH, DEC_SEQ, D_MODEL), 1.0),
        "cache_k": nrm(2, (DEC_BATCH, L, B_KV_HEADS, PAST_LEN, HEAD_DIM), 1.0),
        "cache_v": nrm(3, (DEC_BATCH, L, B_KV_HEADS, PAST_LEN, HEAD_DIM), 1.0),
        "state_rwkv": nrm(4, (DEC_BATCH, L, 2, A_HEADS, HEAD_DIM, HEAD_DIM), 0.5),
        "c": nrm(5, (DEC_BATCH, D_MODEL), 1.0),
        "c_ctx": nrm(6, (D_MODEL,), 1.0),
        "norm_w": 1.0 + nrm(7, (L, D_MODEL), 0.02),
        "w_ada": nrm(8, (L, D_MODEL, 3 * D_MODEL), 0.5 * D_MODEL ** -0.5),
        "b_ada": nrm(9, (L, 3 * D_MODEL), 0.01),
        "w_in": nrm(10, (L, D_MODEL, IN_WIDTH), D_MODEL ** -0.5),
        "conv_a": jnp.array([0.0, 1.0, 0.0], F32)[None, :, None] + nrm(11, (L, CONV_W, A_COLS), 0.1),
        "w0": jax.random.uniform(ks[12], (L, 2, A_WIDTH), F32, -4.0, 1.0),
        "w_up": nrm(13, (L, 2, DECAY_LORA, A_WIDTH), 0.1),
        "a0": nrm(14, (L, 2, A_WIDTH), 0.5),
        "a_up": nrm(15, (L, 2, ICLR_LORA, A_WIDTH), 0.1),
        "k_k": 0.85 + nrm(16, (L, A_WIDTH), 0.05),
        "k_a": 1.0 + nrm(17, (L, A_WIDTH), 0.05),
        "r_k": nrm(18, (L, A_HEADS, HEAD_DIM), 0.1),
        "ln_x_w": 1.0 + nrm(19, (L, A_WIDTH), 0.02),
        "ln_x_b": nrm(20, (L, A_WIDTH), 0.01),
        "w_oA": nrm(21, (L, A_WIDTH, D_MODEL), A_WIDTH ** -0.5),
        "sink": nrm(22, (L, B_KV_HEADS, B_GROUPS), 0.5),
        "w_oB": nrm(23, (L, B_WIDTH, D_MODEL), B_WIDTH ** -0.5),
        "w_out": nrm(24, (L, D_MODEL, D_MODEL), D_MODEL ** -0.5),
        "final_norm_w": 1.0 + nrm(25, (D_MODEL,), 0.02),
    }


def reference(x_prompt, x_sample, cache_k, cache_v, state_rwkv, c, c_ctx, norm_w, w_ada, b_ada,
              w_in, conv_a, w0, w_up, a0, a_up, k_k, k_a, r_k, ln_x_w, ln_x_b, w_oA, sink, w_oB,
              w_out, final_norm_w):
    xp, xs = x_prompt, x_sample
    new_k_list, new_v_list, new_s_list = [], [], []
    for l in range(DEPTH):
        p = {
            'norm_w': norm_w[l], 'w_ada': w_ada[l], 'b_ada': b_ada[l], 'w_in': w_in[l],
            'conv_a': conv_a[l], 'w0': w0[l], 'w_up': w_up[l], 'a0': a0[l], 'a_up': a_up[l],
            'k_k': k_k[l], 'k_a': k_a[l], 'r_k': r_k[l], 'ln_x_w': ln_x_w[l], 'ln_x_b': ln_x_b[l],
            'w_oA': w_oA[l], 'sink': sink[l], 'w_oB': w_oB[l], 'w_out': w_out[l],
        }
        xp, k_c, v_c, s_c = _context_layer(xp, c_ctx, p)
        new_k_list.append(k_c)
        new_v_list.append(v_c)
        new_s_list.append(s_c)
        xs = _latent_layer(xs, c, cache_k[:, l], cache_v[:, l], state_rwkv[:, l], p)
    y_prompt = _rmsnorm(xp, final_norm_w)
    y_sample = _rmsnorm(xs, final_norm_w)
    new_k = jnp.stack(new_k_list, axis=1)
    new_v = jnp.stack(new_v_list, axis=1)
    new_state_rwkv = jnp.stack(new_s_list, axis=1)
    return (y_prompt, y_sample, new_k, new_v, new_state_rwkv)
```

```python
import functools

import jax
import jax.numpy as jnp
from jax import lax
from jax.experimental import pallas as pl
from jax.experimental.pallas import tpu as pltpu

F32 = jnp.float32
BF16 = jnp.bfloat16

D_MODEL = 1024
HEAD_DIM = 64
LANES = 128
N_PAIRS = D_MODEL // LANES
KV_HEADS = 4
KV_WIDTH = KV_HEADS * HEAD_DIM
LORA = 64
GRID_W = 64
WINDOW = 128
QBLK = 128
ROPE_BASE = 10000.0
RMS_EPS = 1e-6
GN_EPS = 64e-5
NEG_INF = -1e30
ATTN_SCALE = HEAD_DIM ** -0.5
CHUNK = 64

COL_R, COL_K, COL_V, COL_GA, COL_Q, COL_GB, COL_MA, COL_MB = (i * D_MODEL for i in range(8))
COL_KB = 8 * D_MODEL
COL_VB = COL_KB + KV_WIDTH
COL_WA = COL_VB + KV_WIDTH
ZW = COL_WA + 2 * LORA
TN_IN = 2944

VMEM_LIMIT = 48 * 1024 * 1024


def _cparams(*sem):
    return pltpu.CompilerParams(dimension_semantics=sem, vmem_limit_bytes=VMEM_LIMIT)


def _sigmoid(x):
    return jax.nn.sigmoid(x)


def _mm(a, b):
    return jnp.dot(a.astype(BF16), b.astype(BF16), preferred_element_type=F32)


def _mm_nt(a, b):
    return lax.dot_general(a.astype(BF16), b.astype(BF16), (((1,), (1,)), ((), ())),
                           preferred_element_type=F32)


def _mm_tn(a, b):
    return lax.dot_general(a.astype(BF16), b.astype(BF16), (((0,), (0,)), ((), ())),
                           preferred_element_type=F32)


def _mod_kernel(c_ref, w_ref, b_ref, o_ref):
    c = c_ref[...]
    o_ref[...] = jnp.dot(c * _sigmoid(c), w_ref[...], preferred_element_type=F32) + b_ref[...]


def _modulation(cond8, w_ada, b_ada):
    n = w_ada.shape[1]
    tn = D_MODEL
    return pl.pallas_call(
        _mod_kernel,
        grid=(n // tn,),
        in_specs=[pl.BlockSpec((8, D_MODEL), lambda j: (0, 0)),
                  pl.BlockSpec((D_MODEL, tn), lambda j: (0, j)),
                  pl.BlockSpec((1, tn), lambda j: (0, j))],
        out_specs=pl.BlockSpec((8, tn), lambda j: (0, j)),
        out_shape=jax.ShapeDtypeStruct((8, n), F32),
        compiler_params=_cparams("arbitrary"),
        name="mod",
    )(cond8, w_ada, b_ada.reshape(1, n))


def _inproj_kernel(x_ref, mod_ref, nw_ref, w_ref, o_ref):
    x = x_ref[...]
    y = x * lax.rsqrt(jnp.mean(x * x, axis=-1, keepdims=True) + RMS_EPS) * nw_ref[...]
    h = y * (1.0 + mod_ref[0, 1:2, :]) + mod_ref[0, 0:1, :]
    o_ref[...] = jnp.dot(h.astype(BF16), w_ref[...], preferred_element_type=F32)


def _inproj(x2, mod3, norm_w, w_in_p, mod_row, tm):
    m = x2.shape[0]
    return pl.pallas_call(
        _inproj_kernel,
        grid=(ZW // TN_IN, m // tm),
        in_specs=[pl.BlockSpec((tm, D_MODEL), lambda n, i: (i, 0)),
                  pl.BlockSpec((1, 3, D_MODEL), lambda n, i: (mod_row(i), 0, 0)),
                  pl.BlockSpec((1, D_MODEL), lambda n, i: (0, 0)),
                  pl.BlockSpec((D_MODEL, TN_IN), lambda n, i: (0, n))],
        out_specs=pl.BlockSpec((tm, TN_IN), lambda n, i: (i, n)),
        out_shape=jax.ShapeDtypeStruct((m, ZW), F32),
        compiler_params=_cparams("arbitrary", "arbitrary"),
        name="inproj",
    )(x2, mod3, norm_w.reshape(1, D_MODEL), w_in_p)


def _segsum(x, e_bd):
    hi = x.astype(BF16)
    lo = (x - hi.astype(F32)).astype(BF16)
    return (jnp.dot(hi, e_bd, preferred_element_type=F32)
            + jnp.dot(lo, e_bd, preferred_element_type=F32))


def _softplus(x):
    return jnp.maximum(x, 0.0) + jnp.log1p(jnp.exp(-jnp.abs(x)))


def _chunk_local(r, kd, v, a, b, lw, rev):
    c = CHUNK
    ti = lax.broadcasted_iota(jnp.int32, (c, c), 0)
    tj = lax.broadcasted_iota(jnp.int32, (c, c), 1)
    tri = jnp.where((tj >= ti) if rev else (tj <= ti), 1.0, 0.0).astype(BF16)
    lw_hi = lw.astype(BF16)
    lw_lo = (lw - lw_hi.astype(F32)).astype(BF16)
    cs = (jnp.dot(tri, lw_hi, preferred_element_type=F32)
          + jnp.dot(tri, lw_lo, preferred_element_type=F32))
    tot = cs[0:1] if rev else cs[c - 1:c]
    e_in = jnp.exp(cs)
    e_neg = jnp.exp(-cs)
    e_end = jnp.exp(tot - cs)
    at = a * jnp.exp(cs - lw)
    rt = r * e_in
    bt = b * e_neg
    kt = kd * e_neg
    bh = b * e_end
    kh = kd * e_end
    pc = jnp.exp(tot)

    lane = lax.broadcasted_iota(jnp.int32, (c, LANES), 1)
    h0 = lane < HEAD_DIM

    def stack2(x):
        return jnp.concatenate([jnp.where(h0, x, 0.0), jnp.where(h0, 0.0, x)], axis=0)

    def collapse(x2):
        return x2[:c] + x2[c:]

    at2 = stack2(at)
    ar2 = jnp.concatenate([at2, stack2(rt)], axis=0).astype(BF16)
    bt16 = bt.astype(BF16)
    kt16 = kt.astype(BF16)
    g_b = _mm_nt(ar2, jnp.concatenate([bt16, bt16], axis=0))
    g_k = _mm_nt(ar2, jnp.concatenate([kt16, kt16], axis=0))

    ri = lax.broadcasted_iota(jnp.int32, (LANES, LANES), 0)
    ci = lax.broadcasted_iota(jnp.int32, (LANES, LANES), 1)
    same = (ri >> 6) == (ci >> 6)
    rt_i = ri & (c - 1)
    ct_i = ci & (c - 1)
    strict = same & ((ct_i > rt_i) if rev else (ct_i < rt_i))
    incl = same & ((ct_i >= rt_i) if rev else (ct_i <= rt_i))
    diag = ri == ci

    a_ab = jnp.where(strict, g_b[0:2 * c], 0.0)
    a_ak = jnp.where(strict, g_k[0:2 * c], 0.0)
    a_rb = jnp.where(incl, g_b[2 * c:4 * c], 0.0)
    a_rk = jnp.where(incl, g_k[2 * c:4 * c], 0.0)

    def blk(shift):
        return (ri >> shift) == (ci >> shift)

    eye = jnp.where(diag, 1.0, 0.0)
    a8 = jnp.where(blk(3), a_ab, 0.0)
    a8_2 = _mm(a8, a8)
    a8_4 = _mm(a8_2, a8_2)
    tm = _mm(_mm(eye + a8, eye + a8_2), eye + a8_4)
    for sh in (3, 4, 5):
        off = jnp.where(blk(sh + 1) & jnp.logical_not(blk(sh)), a_ab, 0.0)
        tm = tm + _mm(_mm(tm, off), tm)

    v2 = stack2(v)
    x = _mm(tm, jnp.concatenate([at2, _mm(a_ak, v2)], axis=1))
    qy = _mm(a_rb, x)
    qeff = rt + collapse(qy[:, :LANES])
    yloc = collapse(qy[:, LANES:] + _mm(a_rk, v2))
    wu = collapse(x)
    md = _mm_tn(jnp.concatenate([bh, kh], axis=0),
                jnp.concatenate([wu, jnp.concatenate([jnp.zeros_like(v), v], axis=1)], axis=0))
    mt = jnp.where(same, md[:, :LANES], 0.0) + jnp.where(diag, pc, 0.0)
    dt = jnp.where(same, md[:, LANES:], 0.0)
    return qeff, yloc, mt, dt


def _rwkv_kernel(*refs, has_s0):
    if has_s0:
        (r_ref, k_ref, v_ref, wa_ref, cr_ref, ck_ref, cv_ref, cwa_ref, pch_ref, lora_ref, s0_ref,
         ya_ref, so_ref, r_s, v_s, a_s, bon_s, kd_s, b_s, lw_s, y_s, q_s, mt_s, dt_s, z_s) = refs
    else:
        (r_ref, k_ref, v_ref, wa_ref, cr_ref, ck_ref, cv_ref, cwa_ref, pch_ref, lora_ref,
         ya_ref, so_ref, r_s, v_s, a_s, bon_s, kd_s, b_s, lw_s, y_s, q_s, mt_s, dt_s, z_s) = refs
        s0_ref = None
    t = r_ref.shape[1]
    nc = t // CHUNK
    row = lax.broadcasted_iota(jnp.int32, (t, LANES), 0)
    lane = lax.broadcasted_iota(jnp.int32, (t, LANES), 1)
    h0 = lane < HEAD_DIM
    ri = lax.broadcasted_iota(jnp.int32, (LANES, LANES), 0)
    ci = lax.broadcasted_iota(jnp.int32, (LANES, LANES), 1)
    e_bd = jnp.where((ri >> 6) == (ci >> 6), 1.0, 0.0).astype(BF16)

    def conv(x, cw_ref):
        prev = jnp.where(row == 0, 0.0, pltpu.roll(x, 1, axis=0))
        nxt = jnp.where(row == t - 1, 0.0, pltpu.roll(x, t - 1, axis=0))
        return cw_ref[0:1, :] * prev + cw_ref[1:2, :] * x + cw_ref[2:3, :] * nxt

    r = conv(r_ref[0], cr_ref)
    k = conv(k_ref[0], ck_ref)
    v = conv(v_ref[0], cv_ref)
    wa = conv(wa_ref[0], cwa_ref)
    k_k, k_a, r_k = pch_ref[0:1, :], pch_ref[1:2, :], pch_ref[2:3, :]
    kk = k * k_k
    kk = kk * lax.rsqrt(jnp.maximum(_segsum(kk * kk, e_bd), 1e-12))
    lora_in = jnp.where(h0, jnp.tanh(wa), wa).astype(BF16)
    r_s[...] = r
    v_s[...] = v
    a_s[...] = -kk
    bon_s[...] = _segsum(r * k * r_k, e_bd) * v
    for d in range(2):
        lo = jnp.dot(lora_in, lora_ref[d, 0], preferred_element_type=F32)
        w_log = -_softplus(-(pch_ref[5 + d:6 + d, :] + lo[:, :LANES])) - 0.5
        a_sig = _sigmoid(pch_ref[7 + d:8 + d, :] + lo[:, LANES:])
        lw_s[d] = -jnp.exp(w_log)
        kd_s[d] = k * (1.0 + (a_sig - 1.0) * k_a)
        b_s[d] = kk * a_sig

    def local_body(c, carry):
        sl = pl.ds(pl.multiple_of(c * CHUNK, CHUNK), CHUNK)
        rc, vc, ac = r_s[sl, :], v_s[sl, :], a_s[sl, :]
        for d in range(2):
            qeff, yloc, mt, dt = _chunk_local(rc, kd_s[d, sl, :], vc, ac, b_s[d, sl, :], lw_s[d, sl, :],
                                              rev=(d == 1))
            q_s[d, sl, :] = qeff
            y_s[d, sl, :] = yloc
            mt_s[d, c] = mt
            dt_s[d, c] = dt
        return carry

    lax.fori_loop(0, nc, local_body, 0)

    def state_in(d):
        if s0_ref is None:
            return jnp.zeros((LANES, LANES), F32)
        sp = s0_ref[0, d, 0]
        hl = lax.broadcasted_iota(jnp.int32, (HEAD_DIM, LANES), 1) < HEAD_DIM
        zt = jnp.concatenate([jnp.where(hl, sp, 0.0), jnp.where(hl, 0.0, sp)], axis=0)
        return zt.T

    for d in range(2):
        z_s[d] = state_in(d)

    def seq_body(i, carry):
        for d in range(2):
            c = i if d == 0 else nc - 1 - i
            sl = pl.ds(pl.multiple_of(c * CHUNK, CHUNK), CHUNK)
            zb = z_s[d].astype(BF16)
            y_s[d, sl, :] = y_s[d, sl, :] + jnp.dot(q_s[d, sl, :].astype(BF16), zb,
                                                    preferred_element_type=F32)
            z_s[d] = jnp.dot(mt_s[d, c].astype(BF16), zb, preferred_element_type=F32) + dt_s[d, c]
        return carry

    lax.fori_loop(0, nc, seq_body, 0)
    for d in range(2):
        zt = z_s[d].T
        so_ref[0, d, 0] = zt[:HEAD_DIM] + zt[HEAD_DIM:]

    y = y_s[0] + y_s[1]
    mu = _segsum(y, e_bd) * (1.0 / HEAD_DIM)
    yc = y - mu
    var = _segsum(yc * yc, e_bd) * (1.0 / HEAD_DIM)
    ya_ref[0] = yc * lax.rsqrt(var + GN_EPS) * pch_ref[3:4, :] + pch_ref[4:5, :] + bon_s[...]


def _rwkv(z3, conv_rkv, conv_wa, pch, lora, s0p):
    bsz, t, _ = z3.shape
    nc = t // CHUNK
    has_s0 = s0p is not None

    def zspec(col):
        return pl.BlockSpec((1, t, LANES), lambda b, j, col=col: (b, 0, col // LANES + j))

    def cspec(col):
        return pl.BlockSpec((3, LANES), lambda b, j, col=col: (0, col // LANES + j))

    in_specs = [zspec(COL_R), zspec(COL_K), zspec(COL_V),
                pl.BlockSpec((1, t, LANES), lambda b, j: (b, 0, COL_WA // LANES)),
                cspec(COL_R), cspec(COL_K), cspec(COL_V),
                pl.BlockSpec((3, LANES), lambda b, j: (0, 0)),
                pl.BlockSpec((16, LANES), lambda b, j: (0, j)),
                pl.BlockSpec((2, 1, LANES, 2 * LANES), lambda b, j: (0, j, 0, 0))]
    args = [z3, z3, z3, z3, conv_rkv, conv_rkv, conv_rkv, conv_wa, pch, lora]
    if has_s0:
        in_specs.append(pl.BlockSpec((1, 2, 1, HEAD_DIM, LANES), lambda b, j: (b, 0, j, 0, 0)))
        args.append(s0p)
    tl = (t, LANES)
    scratch = [pltpu.VMEM(tl, F32)] * 4 + [pltpu.VMEM((2,) + tl, F32)] * 5 \
        + [pltpu.VMEM((2, nc, LANES, LANES), F32)] * 2 + [pltpu.VMEM((2, LANES, LANES), F32)]
    return pl.pallas_call(
        functools.partial(_rwkv_kernel, has_s0=has_s0),
        grid=(bsz, N_PAIRS),
        in_specs=in_specs,
        out_specs=[pl.BlockSpec((1, t, LANES), lambda b, j: (b, 0, j)),
                   pl.BlockSpec((1, 2, 1, HEAD_DIM, LANES), lambda b, j: (b, 0, j, 0, 0))],
        out_shape=[jax.ShapeDtypeStruct((bsz, t, D_MODEL), F32),
                   jax.ShapeDtypeStruct((bsz, 2, N_PAIRS, HEAD_DIM, LANES), F32)],
        scratch_shapes=scratch,
        compiler_params=_cparams("arbitrary", "arbitrary"),
        name="rwkv_s0" if has_s0 else "rwkv",
    )(*args)


def _split_heads(x, odd):
    lo = lax.broadcasted_iota(jnp.int32, x.shape, 1) < HEAD_DIM
    if odd:
        xb = jnp.where(lo, 0.0, x)
        return pltpu.roll(xb, HEAD_DIM, axis=1), xb
    xa = jnp.where(lo, x, 0.0)
    return xa, pltpu.roll(xa, HEAD_DIM, axis=1)


def _softmax_parts(parts, sink):
    m = jnp.maximum(functools.reduce(jnp.maximum, [jnp.max(p, axis=-1, keepdims=True) for p in parts]), sink)
    es = [jnp.exp(p - m) for p in parts]
    den = functools.reduce(jnp.add, [jnp.sum(e, axis=-1, keepdims=True) for e in es]) + jnp.exp(sink - m)
    return es, 1.0 / den


def _ctx_attn_kernel(sink_ref, q_ref, k_ref, v_ref, o_ref, nk_ref, nv_ref):
    t = q_ref.shape[1]
    k = k_ref[0]
    v = v_ref[0]
    lane_lo = lax.broadcasted_iota(jnp.int32, (t, LANES), 1) < HEAD_DIM
    for hk in range(KV_HEADS):
        nk_ref[0, hk] = k[:, hk * HEAD_DIM:(hk + 1) * HEAD_DIM]
        nv_ref[0, hk] = v[:, hk * HEAD_DIM:(hk + 1) * HEAD_DIM]
    for hk in range(KV_HEADS):
        cb = hk // 2
        ka, kb = _split_heads(k[:, cb * LANES:(cb + 1) * LANES], hk % 2)
        va, vb = _split_heads(v[:, cb * LANES:(cb + 1) * LANES], hk % 2)
        kab = jnp.concatenate([ka, kb], axis=0).astype(BF16)
        vab = jnp.concatenate([va, vb], axis=0).astype(BF16)
        for pp in range(2):
            p = 2 * hk + pp
            q2 = (q_ref[0, :, p * LANES:(p + 1) * LANES] * ATTN_SCALE).astype(BF16)
            s = _mm_nt(q2, kab)
            e0, i0 = _softmax_parts([s[:, :t]], sink_ref[hk, 2 * pp])
            e1, i1 = _softmax_parts([s[:, t:]], sink_ref[hk, 2 * pp + 1])
            o = _mm(jnp.concatenate([e0[0], e1[0]], axis=1), vab)
            o_ref[0, :, p * LANES:(p + 1) * LANES] = o * jnp.where(lane_lo, i0, i1)


def _ctx_attention(sink, z3):
    bsz, t, _ = z3.shape
    return pl.pallas_call(
        _ctx_attn_kernel,
        grid=(bsz,),
        in_specs=[pl.BlockSpec(memory_space=pltpu.SMEM),
                  pl.BlockSpec((1, t, D_MODEL), lambda b: (b, 0, COL_Q // D_MODEL)),
                  pl.BlockSpec((1, t, KV_WIDTH), lambda b: (b, 0, COL_KB // KV_WIDTH)),
                  pl.BlockSpec((1, t, KV_WIDTH), lambda b: (b, 0, COL_VB // KV_WIDTH))],
        out_specs=[pl.BlockSpec((1, t, D_MODEL), lambda b: (b, 0, 0)),
                   pl.BlockSpec((1, KV_HEADS, t, HEAD_DIM), lambda b: (b, 0, 0, 0)),
                   pl.BlockSpec((1, KV_HEADS, t, HEAD_DIM), lambda b: (b, 0, 0, 0))],
        out_shape=[jax.ShapeDtypeStruct((bsz, t, D_MODEL), F32),
                   jax.ShapeDtypeStruct((bsz, KV_HEADS, t, HEAD_DIM), F32),
                   jax.ShapeDtypeStruct((bsz, KV_HEADS, t, HEAD_DIM), F32)],
        compiler_params=_cparams("arbitrary"),
        name="ctx_attn",
    )(sink, z3, z3, z3)


def _rope(x, cos, sin_signed):
    lane = lax.broadcasted_iota(jnp.int32, x.shape, 1)
    partner = jnp.where((lane & 31) < 16, pltpu.roll(x, LANES - 16, axis=1), pltpu.roll(x, 16, axis=1))
    return x * cos + partner * sin_signed


def _lat_attn_kernel(sink_ref, q_ref, k_ref, v_ref, ck_ref, cv_ref, cos_ref, sin_ref, o_ref, kab_s, vab_s):
    t = k_ref.shape[1]
    nb = t // QBLK
    i = pl.program_id(1)

    @pl.when(i == 0)
    def _():
        cos = cos_ref[...]
        sin = sin_ref[...]
        for cb in range(KV_HEADS // 2):
            kr = _rope(k_ref[0, :, cb * LANES:(cb + 1) * LANES], cos, sin)
            vv = v_ref[0, :, cb * LANES:(cb + 1) * LANES]
            for odd in range(2):
                hk = 2 * cb + odd
                ka, kb = _split_heads(kr, odd)
                va, vb = _split_heads(vv, odd)
                for j in range(nb):
                    rows = slice(j * QBLK, (j + 1) * QBLK)
                    kab_s[hk, j, 0:QBLK, :] = ka[rows].astype(BF16)
                    kab_s[hk, j, QBLK:2 * QBLK, :] = kb[rows].astype(BF16)
                    vab_s[hk, j, 0:QBLK, :] = va[rows].astype(BF16)
                    vab_s[hk, j, QBLK:2 * QBLK, :] = vb[rows].astype(BF16)

    qrows = pl.ds(pl.multiple_of(i * QBLK, QBLK), QBLK)
    cos_q = cos_ref[qrows, :]
    sin_q = sin_ref[qrows, :]
    ri = lax.broadcasted_iota(jnp.int32, (QBLK, QBLK), 0)
    ci = lax.broadcasted_iota(jnp.int32, (QBLK, QBLK), 1)
    mask_prev = ci >= ri + jnp.where(i >= 1, 0, QBLK)
    mask_next = ci <= ri - jnp.where(i + 1 < nb, 0, QBLK)
    lane_lo = lax.broadcasted_iota(jnp.int32, (QBLK, LANES), 1) < HEAD_DIM
    jm = jnp.maximum(i - 1, 0)
    jp = jnp.minimum(i + 1, nb - 1)
    for hk in range(KV_HEADS):
        kblk = [kab_s[hk, jm], kab_s[hk, i], kab_s[hk, jp]]
        vblk = [vab_s[hk, jm], vab_s[hk, i], vab_s[hk, jp]]
        ckh = ck_ref[0, hk]
        cvh = cv_ref[0, hk]
        tc = ckh.shape[0] // 2
        for pp in range(2):
            p = 2 * hk + pp
            q2 = (_rope(q_ref[0, :, p * LANES:(p + 1) * LANES], cos_q, sin_q) * ATTN_SCALE).astype(BF16)
            s_loc = [_mm_nt(q2, kb) for kb in kblk]
            s_ctx = _mm_nt(q2, ckh)
            exps, invs = [], []
            for half in range(2):
                cols = slice(half * QBLK, (half + 1) * QBLK)
                parts = [jnp.where(mask_prev, s_loc[0][:, cols], NEG_INF),
                         s_loc[1][:, cols],
                         jnp.where(mask_next, s_loc[2][:, cols], NEG_INF),
                         s_ctx[:, half * tc:(half + 1) * tc]]
                es, inv = _softmax_parts(parts, sink_ref[hk, 2 * pp + half])
                exps.append(es)
                invs.append(inv)
            o = _mm(jnp.concatenate([exps[0][3], exps[1][3]], axis=1), cvh)
            for jj in range(3):
                o = o + _mm(jnp.concatenate([exps[0][jj], exps[1][jj]], axis=1), vblk[jj])
            o_ref[0, :, p * LANES:(p + 1) * LANES] = o * jnp.where(lane_lo, invs[0], invs[1])


def _lat_attention(sink, z3, ck2, cv2, cos_t, sin_t):
    bsz, t, _ = z3.shape
    nb = t // QBLK
    tc2 = ck2.shape[2]
    return pl.pallas_call(
        _lat_attn_kernel,
        grid=(bsz, nb),
        in_specs=[pl.BlockSpec(memory_space=pltpu.SMEM),
                  pl.BlockSpec((1, QBLK, D_MODEL), lambda b, i: (b, i, COL_Q // D_MODEL)),
                  pl.BlockSpec((1, t, KV_WIDTH), lambda b, i: (b, 0, COL_KB // KV_WIDTH)),
                  pl.BlockSpec((1, t, KV_WIDTH), lambda b, i: (b, 0, COL_VB // KV_WIDTH)),
                  pl.BlockSpec((1, KV_HEADS, tc2, LANES), lambda b, i: (b, 0, 0, 0)),
                  pl.BlockSpec((1, KV_HEADS, tc2, LANES), lambda b, i: (b, 0, 0, 0)),
                  pl.BlockSpec((t, LANES), lambda b, i: (0, 0)),
                  pl.BlockSpec((t, LANES), lambda b, i: (0, 0))],
        out_specs=pl.BlockSpec((1, QBLK, D_MODEL), lambda b, i: (b, i, 0)),
        out_shape=jax.ShapeDtypeStruct((bsz, t, D_MODEL), F32),
        scratch_shapes=[pltpu.VMEM((KV_HEADS, nb, 2 * QBLK, LANES), BF16)] * 2,
        compiler_params=_cparams("arbitrary", "arbitrary"),
        name="lat_attn",
    )(sink, z3, z3, z3, ck2, cv2, cos_t, sin_t)


def _merge_kernel(x_ref, mod_ref, ya_ref, yb_ref, ga_ref, gb_ref, ma_ref, mb_ref,
                  woa_ref, wob_ref, wout_ref, fw_ref, o_ref):
    ga = ga_ref[...]
    gb = gb_ref[...]
    br_a = jnp.dot((ya_ref[...] * (ga * _sigmoid(ga))).astype(BF16), woa_ref[...], preferred_element_type=F32)
    br_b = jnp.dot((yb_ref[...] * (gb * _sigmoid(gb))).astype(BF16), wob_ref[...], preferred_element_type=F32)
    merged = _sigmoid(ma_ref[...]) * br_a + _sigmoid(mb_ref[...]) * br_b
    out = x_ref[...] + mod_ref[0, 2:3, :] * jnp.dot(merged.astype(BF16), wout_ref[...],
                                                     preferred_element_type=F32)
    o_ref[...] = out * lax.rsqrt(jnp.mean(out * out, axis=-1, keepdims=True) + RMS_EPS) * fw_ref[...]


def _merge(x2, mod3, ya2, yb2, z2, woa, wob, wout, fw, mod_row, tm):
    m = x2.shape[0]

    def tok(col=0):
        return pl.BlockSpec((tm, D_MODEL), lambda i, col=col: (i, col // D_MODEL))

    def wspec():
        return pl.BlockSpec((D_MODEL, D_MODEL), lambda i: (0, 0))

    return pl.pallas_call(
        _merge_kernel,
        grid=(m // tm,),
        in_specs=[tok(), pl.BlockSpec((1, 3, D_MODEL), lambda i: (mod_row(i), 0, 0)),
                  tok(), tok(), tok(COL_GA), tok(COL_GB), tok(COL_MA), tok(COL_MB),
                  wspec(), wspec(), wspec(), pl.BlockSpec((1, D_MODEL), lambda i: (0, 0))],
        out_specs=tok(),
        out_shape=jax.ShapeDtypeStruct((m, D_MODEL), F32),
        compiler_params=_cparams("arbitrary"),
        name="merge",
    )(x2, mod3, ya2, yb2, z2, z2, z2, z2, woa, wob, wout, fw.reshape(1, D_MODEL))


def _rope_tables(t):
    n_rows = t // GRID_W
    row = jnp.repeat(jnp.arange(n_rows), GRID_W)
    col = jnp.tile(jnp.arange(GRID_W), n_rows)
    nf = HEAD_DIM // 4
    inv = ROPE_BASE ** (-jnp.arange(nf, dtype=F32) / nf)
    ang_r = row.astype(F32)[:, None] * inv[None, :]
    ang_c = col.astype(F32)[:, None] * inv[None, :]
    cos = jnp.concatenate([jnp.cos(ang_r), jnp.cos(ang_r), jnp.cos(ang_c), jnp.cos(ang_c)], axis=-1)
    sin = jnp.concatenate([-jnp.sin(ang_r), jnp.sin(ang_r), -jnp.sin(ang_c), jnp.sin(ang_c)], axis=-1)
    return jnp.tile(cos, (1, 2)), jnp.tile(sin, (1, 2))


def _pad_heads(x):
    zero = jnp.zeros_like(x)
    return jnp.concatenate([jnp.concatenate([x, zero], axis=-1),
                            jnp.concatenate([zero, x], axis=-1)], axis=-2).astype(BF16)


def kernel(x_prompt, x_sample, cache_k, cache_v, state_rwkv, c, c_ctx, norm_w, w_ada, b_ada, w_in, conv_a,
           w0, w_up, a0, a_up, k_k, k_a, r_k, ln_x_w, ln_x_b, w_oA, sink, w_oB, w_out, final_norm_w):
    depth = norm_w.shape[0]
    assert depth == 1
    bc, tc, _ = x_prompt.shape
    bd, td, _ = x_sample.shape
    l = 0
    a_cols = 3 * D_MODEL + 2 * LORA

    wi = w_in[l]
    o_ga = a_cols
    o_q = o_ga + D_MODEL
    o_kb = o_q + D_MODEL
    o_vb = o_kb + KV_WIDTH
    o_gb = o_vb + KV_WIDTH
    o_ma = o_gb + D_MODEL
    o_mb = o_ma + D_MODEL
    w_in_p = jnp.concatenate(
        [wi[:, 0:3 * D_MODEL], wi[:, o_ga:o_q], wi[:, o_q:o_kb], wi[:, o_gb:o_ma], wi[:, o_ma:o_mb],
         wi[:, o_mb:o_mb + D_MODEL], wi[:, o_kb:o_vb], wi[:, o_vb:o_gb], wi[:, 3 * D_MODEL:a_cols]],
        axis=1).astype(BF16)
    conv_rkv = conv_a[l][:, 0:3 * D_MODEL]
    conv_wa = conv_a[l][:, 3 * D_MODEL:a_cols]
    pch = jnp.concatenate(
        [k_k[l][None], k_a[l][None], r_k[l].reshape(1, D_MODEL), ln_x_w[l][None], ln_x_b[l][None],
         w0[l], a0[l], jnp.zeros((7, D_MODEL), F32)], axis=0)
    wup = w_up[l].reshape(2, LORA, N_PAIRS, LANES).transpose(0, 2, 1, 3)
    aup = a_up[l].reshape(2, LORA, N_PAIRS, LANES).transpose(0, 2, 1, 3)
    zl = jnp.zeros_like(wup)
    lora = jnp.concatenate([jnp.concatenate([wup, zl], axis=-1),
                            jnp.concatenate([zl, aup], axis=-1)], axis=-2).astype(BF16)
    woa, wob, wout = w_oA[l].astype(BF16), w_oB[l].astype(BF16), w_out[l].astype(BF16)
    sink_l = sink[l]

    cond8 = jnp.concatenate([c_ctx[None], c, jnp.zeros((8 - 1 - bd, D_MODEL), F32)], axis=0)
    mod3 = _modulation(cond8, w_ada[l], b_ada[l]).reshape(8, 3, D_MODEL)

    tm = 512
    ctx_row = lambda i: 0
    lat_row = lambda i: 1 + (i * tm) // td

    xp2 = x_prompt.reshape(bc * tc, D_MODEL)
    zc = _inproj(xp2, mod3, norm_w[l], w_in_p, ctx_row, tm)
    zc3 = zc.reshape(bc, tc, ZW)
    ya_c, st_c = _rwkv(zc3, conv_rkv, conv_wa, pch, lora, None)
    yb_c, new_k, new_v = _ctx_attention(sink_l, zc3)
    y_prompt = _merge(xp2, mod3, ya_c.reshape(bc * tc, D_MODEL), yb_c.reshape(bc * tc, D_MODEL), zc,
                      woa, wob, wout, final_norm_w, ctx_row, tm).reshape(bc, tc, D_MODEL)
    new_state = st_c.reshape(bc, 2, N_PAIRS, HEAD_DIM, 2, HEAD_DIM).transpose(0, 1, 2, 4, 3, 5)
    new_state = new_state.reshape(bc, 1, 2, 2 * N_PAIRS, HEAD_DIM, HEAD_DIM)

    xs2 = x_sample.reshape(bd * td, D_MODEL)
    zd = _inproj(xs2, mod3, norm_w[l], w_in_p, lat_row, tm)
    zd3 = zd.reshape(bd, td, ZW)
    s0p = state_rwkv[:, l].reshape(bd, 2, N_PAIRS, 2, HEAD_DIM, HEAD_DIM).transpose(0, 1, 2, 4, 3, 5)
    s0p = s0p.reshape(bd, 2, N_PAIRS, HEAD_DIM, LANES)
    ya_d, _ = _rwkv(zd3, conv_rkv, conv_wa, pch, lora, s0p)
    cos_t, sin_t = _rope_tables(td)
    yb_d = _lat_attention(sink_l, zd3, _pad_heads(cache_k[:, l]), _pad_heads(cache_v[:, l]), cos_t, sin_t)
    y_sample = _merge(xs2, mod3, ya_d.reshape(bd * td, D_MODEL), yb_d.reshape(bd * td, D_MODEL), zd,
                      woa, wob, wout, final_norm_w, lat_row, tm).reshape(bd, td, D_MODEL)

    return (y_prompt, y_sample, new_k[:, None], new_v[:, None], new_state)
```

```python
import functools

import jax
import jax.numpy as jnp
from jax import lax
from jax.experimental import pallas as pl
from jax.experimental.pallas import tpu as pltpu

F32 = jnp.float32
BF16 = jnp.bfloat16

D_MODEL = 1024
HEAD_DIM = 64
LANES = 128
N_PAIRS = D_MODEL // LANES
KV_HEADS = 4
KV_WIDTH = KV_HEADS * HEAD_DIM
LORA = 64
GRID_W = 64
WINDOW = 128
QBLK = 128
ROPE_BASE = 10000.0
RMS_EPS = 1e-6
GN_EPS = 64e-5
NEG_INF = -1e30
ATTN_SCALE = HEAD_DIM ** -0.5
GROUP = 4
CHUNK = 64

COL_R, COL_K, COL_V, COL_GA, COL_Q, COL_GB, COL_MA, COL_MB = (i * D_MODEL for i in range(8))
COL_KB = 8 * D_MODEL
COL_VB = COL_KB + KV_WIDTH
COL_WA = COL_VB + KV_WIDTH
ZW = COL_WA + 2 * LORA
TN_IN = 2944

VMEM_LIMIT = 48 * 1024 * 1024


def _cparams(*sem):
    return pltpu.CompilerParams(dimension_semantics=sem, vmem_limit_bytes=VMEM_LIMIT)


def _sigmoid(x):
    return jax.nn.sigmoid(x)


def _mm(a, b):
    return jnp.dot(a.astype(BF16), b.astype(BF16), preferred_element_type=F32)


def _mm_nt(a, b):
    return lax.dot_general(a.astype(BF16), b.astype(BF16), (((1,), (1,)), ((), ())),
                           preferred_element_type=F32)


def _mm_tn(a, b):
    return lax.dot_general(a.astype(BF16), b.astype(BF16), (((0,), (0,)), ((), ())),
                           preferred_element_type=F32)


def _mod_kernel(c_ref, w_ref, b_ref, o_ref):
    c = c_ref[...]
    o_ref[...] = jnp.dot(c * _sigmoid(c), w_ref[...], preferred_element_type=F32) + b_ref[...]


def _modulation(cond8, w_ada, b_ada):
    n = w_ada.shape[1]
    tn = D_MODEL
    return pl.pallas_call(
        _mod_kernel,
        grid=(n // tn,),
        in_specs=[pl.BlockSpec((8, D_MODEL), lambda j: (0, 0)),
                  pl.BlockSpec((D_MODEL, tn), lambda j: (0, j)),
                  pl.BlockSpec((1, tn), lambda j: (0, j))],
        out_specs=pl.BlockSpec((8, tn), lambda j: (0, j)),
        out_shape=jax.ShapeDtypeStruct((8, n), F32),
        compiler_params=_cparams("arbitrary"),
        name="mod",
    )(cond8, w_ada, b_ada.reshape(1, n))


def _inproj_kernel(x_ref, mod_ref, nw_ref, w_ref, o_ref):
    x = x_ref[...]
    y = x * lax.rsqrt(jnp.mean(x * x, axis=-1, keepdims=True) + RMS_EPS) * nw_ref[...]
    h = y * (1.0 + mod_ref[0, 1:2, :]) + mod_ref[0, 0:1, :]
    o_ref[...] = jnp.dot(h.astype(BF16), w_ref[...], preferred_element_type=F32)


def _inproj(x2, mod3, norm_w, w_in_p, mod_row, tm):
    m = x2.shape[0]
    return pl.pallas_call(
        _inproj_kernel,
        grid=(ZW // TN_IN, m // tm),
        in_specs=[pl.BlockSpec((tm, D_MODEL), lambda n, i: (i, 0)),
                  pl.BlockSpec((1, 3, D_MODEL), lambda n, i: (mod_row(i), 0, 0)),
                  pl.BlockSpec((1, D_MODEL), lambda n, i: (0, 0)),
                  pl.BlockSpec((D_MODEL, TN_IN), lambda n, i: (0, n))],
        out_specs=pl.BlockSpec((tm, TN_IN), lambda n, i: (i, n)),
        out_shape=jax.ShapeDtypeStruct((m, ZW), F32),
        compiler_params=_cparams("arbitrary", "arbitrary"),
        name="inproj",
    )(x2, mod3, norm_w.reshape(1, D_MODEL), w_in_p)


def _segsum(x, e_bd):
    hi = x.astype(BF16)
    lo = (x - hi.astype(F32)).astype(BF16)
    return (jnp.dot(hi, e_bd, preferred_element_type=F32)
            + jnp.dot(lo, e_bd, preferred_element_type=F32))


def _softplus(x):
    return jnp.maximum(x, 0.0) + jnp.log1p(jnp.exp(-jnp.abs(x)))


def _bmm(a, b):
    return jnp.einsum("bij,bjk->bik", a.astype(BF16), b.astype(BF16), preferred_element_type=F32)


def _bmm_nt(a, b):
    return jnp.einsum("bik,bjk->bij", a.astype(BF16), b.astype(BF16), preferred_element_type=F32)


def _bmm_tn(a, b):
    return jnp.einsum("bki,bkj->bij", a.astype(BF16), b.astype(BF16), preferred_element_type=F32)


def _chunk_local(r, kd, v, a, b, lw):
    c = CHUNK
    n = r.shape[0]

    def rev_like(shape):
        return lax.broadcasted_iota(jnp.int32, shape, 0) >= n // 2

    ti = lax.broadcasted_iota(jnp.int32, (n, c, c), 1)
    tj = lax.broadcasted_iota(jnp.int32, (n, c, c), 2)
    tri = jnp.where(jnp.where(rev_like((n, c, c)), tj - ti, ti - tj) >= 0, 1.0, 0.0).astype(BF16)
    lw_hi = lw.astype(BF16)
    lw_lo = (lw - lw_hi.astype(F32)).astype(BF16)
    cs = _bmm(tri, lw_hi) + _bmm(tri, lw_lo)
    tot = jnp.where(rev_like((n, 1, LANES)), cs[:, 0:1], cs[:, c - 1:c])
    e_in = jnp.exp(cs)
    e_neg = jnp.exp(-cs)
    e_end = jnp.exp(tot - cs)
    at = a * jnp.exp(cs - lw)
    rt = r * e_in
    bt = b * e_neg
    kt = kd * e_neg
    bh = b * e_end
    kh = kd * e_end
    pc = jnp.exp(tot)

    h0 = lax.broadcasted_iota(jnp.int32, (n, c, LANES), 2) < HEAD_DIM

    def stack2(x):
        return jnp.concatenate([jnp.where(h0, x, 0.0), jnp.where(h0, 0.0, x)], axis=1)

    def collapse(x2):
        return x2[:, :c] + x2[:, c:]

    at2 = stack2(at)
    ar2 = jnp.concatenate([at2, stack2(rt)], axis=1).astype(BF16)
    bt16 = bt.astype(BF16)
    kt16 = kt.astype(BF16)
    g_b = _bmm_nt(ar2, jnp.concatenate([bt16, bt16], axis=1))
    g_k = _bmm_nt(ar2, jnp.concatenate([kt16, kt16], axis=1))

    sq = (n, LANES, LANES)
    ri = lax.broadcasted_iota(jnp.int32, sq, 1)
    ci = lax.broadcasted_iota(jnp.int32, sq, 2)
    rev = rev_like(sq)
    same = (ri >> 6) == (ci >> 6)
    ahead = (ri & (c - 1)) - (ci & (c - 1))
    ahead = jnp.where(rev, -ahead, ahead)
    strict = same & (ahead > 0)
    incl = same & (ahead >= 0)
    diag = ri == ci

    a_ab = jnp.where(strict, g_b[:, 0:2 * c], 0.0)
    a_ak = jnp.where(strict, g_k[:, 0:2 * c], 0.0)
    a_rb = jnp.where(incl, g_b[:, 2 * c:4 * c], 0.0)
    a_rk = jnp.where(incl, g_k[:, 2 * c:4 * c], 0.0)

    def blk(shift):
        return (ri >> shift) == (ci >> shift)

    eye = jnp.where(diag, 1.0, 0.0)
    a8 = jnp.where(blk(3), a_ab, 0.0)
    a8_2 = _bmm(a8, a8)
    a8_4 = _bmm(a8_2, a8_2)
    tm = _bmm(_bmm(eye + a8, eye + a8_2), eye + a8_4)
    for sh in (3, 4, 5):
        off = jnp.where(blk(sh + 1) & jnp.logical_not(blk(sh)), a_ab, 0.0)
        tm = tm + _bmm(_bmm(tm, off), tm)

    v2 = stack2(v)
    x = _bmm(tm, jnp.concatenate([at2, _bmm(a_ak, v2)], axis=2))
    qy = _bmm(a_rb, x)
    qeff = rt + collapse(qy[:, :, :LANES])
    yloc = collapse(qy[:, :, LANES:] + _bmm(a_rk, v2))
    wu = collapse(x)
    md = _bmm_tn(jnp.concatenate([bh, kh], axis=1),
                 jnp.concatenate([wu, jnp.concatenate([jnp.zeros_like(v), v], axis=2)], axis=1))
    mt = jnp.where(same, md[:, :, :LANES], 0.0) + jnp.where(diag, pc, 0.0)
    dt = jnp.where(same, md[:, :, LANES:], 0.0)
    return qeff, yloc, mt, dt


def _rwkv_kernel(*refs, has_s0):
    if has_s0:
        (r_ref, k_ref, v_ref, wa_ref, cr_ref, ck_ref, cv_ref, cwa_ref, pch_ref, lora_ref, s0_ref,
         ya_ref, so_ref, r_s, v_s, a_s, bon_s, kd_s, b_s, lw_s, y_s, q_s, mt_s, dt_s, z_s) = refs
    else:
        (r_ref, k_ref, v_ref, wa_ref, cr_ref, ck_ref, cv_ref, cwa_ref, pch_ref, lora_ref,
         ya_ref, so_ref, r_s, v_s, a_s, bon_s, kd_s, b_s, lw_s, y_s, q_s, mt_s, dt_s, z_s) = refs
        s0_ref = None
    t = r_ref.shape[1]
    nc = t // CHUNK
    row = lax.broadcasted_iota(jnp.int32, (t, LANES), 0)
    lane = lax.broadcasted_iota(jnp.int32, (t, LANES), 1)
    h0 = lane < HEAD_DIM
    ri = lax.broadcasted_iota(jnp.int32, (LANES, LANES), 0)
    ci = lax.broadcasted_iota(jnp.int32, (LANES, LANES), 1)
    e_bd = jnp.where((ri >> 6) == (ci >> 6), 1.0, 0.0).astype(BF16)

    def conv(x, cw_ref):
        prev = jnp.where(row == 0, 0.0, pltpu.roll(x, 1, axis=0))
        nxt = jnp.where(row == t - 1, 0.0, pltpu.roll(x, t - 1, axis=0))
        return cw_ref[0:1, :] * prev + cw_ref[1:2, :] * x + cw_ref[2:3, :] * nxt

    r = conv(r_ref[0], cr_ref)
    k = conv(k_ref[0], ck_ref)
    v = conv(v_ref[0], cv_ref)
    wa = conv(wa_ref[0], cwa_ref)
    k_k, k_a, r_k = pch_ref[0:1, :], pch_ref[1:2, :], pch_ref[2:3, :]
    kk = k * k_k
    kk = kk * lax.rsqrt(jnp.maximum(_segsum(kk * kk, e_bd), 1e-12))
    lora_in = jnp.where(h0, jnp.tanh(wa), wa).astype(BF16)
    r_s[...] = r
    v_s[...] = v
    a_s[...] = -kk
    bon_s[...] = _segsum(r * k * r_k, e_bd) * v
    for d in range(2):
        lo = jnp.dot(lora_in, lora_ref[d, 0], preferred_element_type=F32)
        w_log = -_softplus(-(pch_ref[5 + d:6 + d, :] + lo[:, :LANES])) - 0.5
        a_sig = _sigmoid(pch_ref[7 + d:8 + d, :] + lo[:, LANES:])
        lw_s[d] = -jnp.exp(w_log)
        kd_s[d] = k * (1.0 + (a_sig - 1.0) * k_a)
        b_s[d] = kk * a_sig

    gsz = GROUP * CHUNK

    def local_body(g, carry):
        rows = pl.ds(pl.multiple_of(g * gsz, gsz), gsz)
        cidx = pl.ds(pl.multiple_of(g * GROUP, GROUP), GROUP)

        def both(shared):
            x = shared.reshape(GROUP, CHUNK, LANES)
            return jnp.concatenate([x, x], axis=0)

        def per_dir(ref):
            return jnp.concatenate([ref[0, rows, :].reshape(GROUP, CHUNK, LANES),
                                    ref[1, rows, :].reshape(GROUP, CHUNK, LANES)], axis=0)

        qeff, yloc, mt, dt = _chunk_local(both(r_s[rows, :]), per_dir(kd_s), both(v_s[rows, :]),
                                          both(a_s[rows, :]), per_dir(b_s), per_dir(lw_s))
        for d in range(2):
            part = slice(d * GROUP, (d + 1) * GROUP)
            q_s[d, rows, :] = qeff[part].reshape(gsz, LANES)
            y_s[d, rows, :] = yloc[part].reshape(gsz, LANES)
            mt_s[d, cidx] = mt[part]
            dt_s[d, cidx] = dt[part]
        return carry

    lax.fori_loop(0, nc // GROUP, local_body, 0)

    def state_in(d):
        if s0_ref is None:
            return jnp.zeros((LANES, LANES), F32)
        sp = s0_ref[0, d, 0]
        hl = lax.broadcasted_iota(jnp.int32, (HEAD_DIM, LANES), 1) < HEAD_DIM
        zt = jnp.concatenate([jnp.where(hl, sp, 0.0), jnp.where(hl, 0.0, sp)], axis=0)
        return zt.T

    for d in range(2):
        z_s[d] = state_in(d)

    def seq_body(i, carry):
        for d in range(2):
            c = i if d == 0 else nc - 1 - i
            sl = pl.ds(pl.multiple_of(c * CHUNK, CHUNK), CHUNK)
            zb = z_s[d].astype(BF16)
            y_s[d, sl, :] = y_s[d, sl, :] + jnp.dot(q_s[d, sl, :].astype(BF16), zb,
                                                    preferred_element_type=F32)
            z_s[d] = jnp.dot(mt_s[d, c].astype(BF16), zb, preferred_element_type=F32) + dt_s[d, c]
        return carry

    lax.fori_loop(0, nc, seq_body, 0)
    for d in range(2):
        zt = z_s[d].T
        so_ref[0, d, 0] = zt[:HEAD_DIM] + zt[HEAD_DIM:]

    y = y_s[0] + y_s[1]
    mu = _segsum(y, e_bd) * (1.0 / HEAD_DIM)
    yc = y - mu
    var = _segsum(yc * yc, e_bd) * (1.0 / HEAD_DIM)
    ya_ref[0] = yc * lax.rsqrt(var + GN_EPS) * pch_ref[3:4, :] + pch_ref[4:5, :] + bon_s[...]


def _rwkv(z3, conv_rkv, conv_wa, pch, lora, s0p):
    bsz, t, _ = z3.shape
    nc = t // CHUNK
    has_s0 = s0p is not None

    def zspec(col):
        return pl.BlockSpec((1, t, LANES), lambda b, j, col=col: (b, 0, col // LANES + j))

    def cspec(col):
        return pl.BlockSpec((3, LANES), lambda b, j, col=col: (0, col // LANES + j))

    in_specs = [zspec(COL_R), zspec(COL_K), zspec(COL_V),
                pl.BlockSpec((1, t, LANES), lambda b, j: (b, 0, COL_WA // LANES)),
                cspec(COL_R), cspec(COL_K), cspec(COL_V),
                pl.BlockSpec((3, LANES), lambda b, j: (0, 0)),
                pl.BlockSpec((16, LANES), lambda b, j: (0, j)),
                pl.BlockSpec((2, 1, LANES, 2 * LANES), lambda b, j: (0, j, 0, 0))]
    args = [z3, z3, z3, z3, conv_rkv, conv_rkv, conv_rkv, conv_wa, pch, lora]
    if has_s0:
        in_specs.append(pl.BlockSpec((1, 2, 1, HEAD_DIM, LANES), lambda b, j: (b, 0, j, 0, 0)))
        args.append(s0p)
    tl = (t, LANES)
    scratch = [pltpu.VMEM(tl, F32)] * 4 + [pltpu.VMEM((2,) + tl, F32)] * 5 \
        + [pltpu.VMEM((2, nc, LANES, LANES), F32)] * 2 + [pltpu.VMEM((2, LANES, LANES), F32)]
    return pl.pallas_call(
        functools.partial(_rwkv_kernel, has_s0=has_s0),
        grid=(bsz, N_PAIRS),
        in_specs=in_specs,
        out_specs=[pl.BlockSpec((1, t, LANES), lambda b, j: (b, 0, j)),
                   pl.BlockSpec((1, 2, 1, HEAD_DIM, LANES), lambda b, j: (b, 0, j, 0, 0))],
        out_shape=[jax.ShapeDtypeStruct((bsz, t, D_MODEL), F32),
                   jax.ShapeDtypeStruct((bsz, 2, N_PAIRS, HEAD_DIM, LANES), F32)],
        scratch_shapes=scratch,
        compiler_params=_cparams("arbitrary", "arbitrary"),
        name="rwkv_s0" if has_s0 else "rwkv",
    )(*args)


def _split_heads(x, odd):
    lo = lax.broadcasted_iota(jnp.int32, x.shape, 1) < HEAD_DIM
    if odd:
        xb = jnp.where(lo, 0.0, x)
        return pltpu.roll(xb, HEAD_DIM, axis=1), xb
    xa = jnp.where(lo, x, 0.0)
    return xa, pltpu.roll(xa, HEAD_DIM, axis=1)


def _softmax_parts(parts, sink):
    m = jnp.maximum(functools.reduce(jnp.maximum, [jnp.max(p, axis=-1, keepdims=True) for p in parts]), sink)
    es = [jnp.exp(p - m) for p in parts]
    den = functools.reduce(jnp.add, [jnp.sum(e, axis=-1, keepdims=True) for e in es]) + jnp.exp(sink - m)
    return es, 1.0 / den


def _ctx_attn_kernel(sink_ref, q_ref, k_ref, v_ref, o_ref, nk_ref, nv_ref):
    t = q_ref.shape[1]
    k = k_ref[0]
    v = v_ref[0]
    lane_lo = lax.broadcasted_iota(jnp.int32, (t, LANES), 1) < HEAD_DIM
    for hk in range(KV_HEADS):
        nk_ref[0, hk] = k[:, hk * HEAD_DIM:(hk + 1) * HEAD_DIM]
        nv_ref[0, hk] = v[:, hk * HEAD_DIM:(hk + 1) * HEAD_DIM]
    for hk in range(KV_HEADS):
        cb = hk // 2
        ka, kb = _split_heads(k[:, cb * LANES:(cb + 1) * LANES], hk % 2)
        va, vb = _split_heads(v[:, cb * LANES:(cb + 1) * LANES], hk % 2)
        kab = jnp.concatenate([ka, kb], axis=0).astype(BF16)
        vab = jnp.concatenate([va, vb], axis=0).astype(BF16)
        for pp in range(2):
            p = 2 * hk + pp
            q2 = (q_ref[0, :, p * LANES:(p + 1) * LANES] * ATTN_SCALE).astype(BF16)
            s = _mm_nt(q2, kab)
            e0, i0 = _softmax_parts([s[:, :t]], sink_ref[hk, 2 * pp])
            e1, i1 = _softmax_parts([s[:, t:]], sink_ref[hk, 2 * pp + 1])
            o = _mm(jnp.concatenate([e0[0], e1[0]], axis=1), vab)
            o_ref[0, :, p * LANES:(p + 1) * LANES] = o * jnp.where(lane_lo, i0, i1)


def _ctx_attention(sink, z3):
    bsz, t, _ = z3.shape
    return pl.pallas_call(
        _ctx_attn_kernel,
        grid=(bsz,),
        in_specs=[pl.BlockSpec(memory_space=pltpu.SMEM),
                  pl.BlockSpec((1, t, D_MODEL), lambda b: (b, 0, COL_Q // D_MODEL)),
                  pl.BlockSpec((1, t, KV_WIDTH), lambda b: (b, 0, COL_KB // KV_WIDTH)),
                  pl.BlockSpec((1, t, KV_WIDTH), lambda b: (b, 0, COL_VB // KV_WIDTH))],
        out_specs=[pl.BlockSpec((1, t, D_MODEL), lambda b: (b, 0, 0)),
                   pl.BlockSpec((1, KV_HEADS, t, HEAD_DIM), lambda b: (b, 0, 0, 0)),
                   pl.BlockSpec((1, KV_HEADS, t, HEAD_DIM), lambda b: (b, 0, 0, 0))],
        out_shape=[jax.ShapeDtypeStruct((bsz, t, D_MODEL), F32),
                   jax.ShapeDtypeStruct((bsz, KV_HEADS, t, HEAD_DIM), F32),
                   jax.ShapeDtypeStruct((bsz, KV_HEADS, t, HEAD_DIM), F32)],
        compiler_params=_cparams("arbitrary"),
        name="ctx_attn",
    )(sink, z3, z3, z3)


def _rope(x, cos, sin_signed):
    lane = lax.broadcasted_iota(jnp.int32, x.shape, 1)
    partner = jnp.where((lane & 31) < 16, pltpu.roll(x, LANES - 16, axis=1), pltpu.roll(x, 16, axis=1))
    return x * cos + partner * sin_signed


def _lat_attn_kernel(sink_ref, q_ref, k_ref, v_ref, ck_ref, cv_ref, cos_ref, sin_ref, o_ref, kab_s, vab_s):
    t = k_ref.shape[1]
    nb = t // QBLK
    i = pl.program_id(1)

    @pl.when(i == 0)
    def _():
        cos = cos_ref[...]
        sin = sin_ref[...]
        for cb in range(KV_HEADS // 2):
            kr = _rope(k_ref[0, :, cb * LANES:(cb + 1) * LANES], cos, sin)
            vv = v_ref[0, :, cb * LANES:(cb + 1) * LANES]
            for odd in range(2):
                hk = 2 * cb + odd
                ka, kb = _split_heads(kr, odd)
                va, vb = _split_heads(vv, odd)
                for j in range(nb):
                    rows = slice(j * QBLK, (j + 1) * QBLK)
                    kab_s[hk, j, 0:QBLK, :] = ka[rows].astype(BF16)
                    kab_s[hk, j, QBLK:2 * QBLK, :] = kb[rows].astype(BF16)
                    vab_s[hk, j, 0:QBLK, :] = va[rows].astype(BF16)
                    vab_s[hk, j, QBLK:2 * QBLK, :] = vb[rows].astype(BF16)

    qrows = pl.ds(pl.multiple_of(i * QBLK, QBLK), QBLK)
    cos_q = cos_ref[qrows, :]
    sin_q = sin_ref[qrows, :]
    ri = lax.broadcasted_iota(jnp.int32, (QBLK, QBLK), 0)
    ci = lax.broadcasted_iota(jnp.int32, (QBLK, QBLK), 1)
    mask_prev = ci >= ri + jnp.where(i >= 1, 0, QBLK)
    mask_next = ci <= ri - jnp.where(i + 1 < nb, 0, QBLK)
    lane_lo = lax.broadcasted_iota(jnp.int32, (QBLK, LANES), 1) < HEAD_DIM
    jm = jnp.maximum(i - 1, 0)
    jp = jnp.minimum(i + 1, nb - 1)
    for hk in range(KV_HEADS):
        kblk = [kab_s[hk, jm], kab_s[hk, i], kab_s[hk, jp]]
        vblk = [vab_s[hk, jm], vab_s[hk, i], vab_s[hk, jp]]
        ckh = ck_ref[0, hk]
        cvh = cv_ref[0, hk]
        tc = ckh.shape[0] // 2
        for pp in range(2):
            p = 2 * hk + pp
            q2 = (_rope(q_ref[0, :, p * LANES:(p + 1) * LANES], cos_q, sin_q) * ATTN_SCALE).astype(BF16)
            s_loc = [_mm_nt(q2, kb) for kb in kblk]
            s_ctx = _mm_nt(q2, ckh)
            exps, invs = [], []
            for half in range(2):
                cols = slice(half * QBLK, (half + 1) * QBLK)
                parts = [jnp.where(mask_prev, s_loc[0][:, cols], NEG_INF),
                         s_loc[1][:, cols],
                         jnp.where(mask_next, s_loc[2][:, cols], NEG_INF),
                         s_ctx[:, half * tc:(half + 1) * tc]]
                es, inv = _softmax_parts(parts, sink_ref[hk, 2 * pp + half])
                exps.append(es)
                invs.append(inv)
            o = _mm(jnp.concatenate([exps[0][3], exps[1][3]], axis=1), cvh)
            for jj in range(3):
                o = o + _mm(jnp.concatenate([exps[0][jj], exps[1][jj]], axis=1), vblk[jj])
            o_ref[0, :, p * LANES:(p + 1) * LANES] = o * jnp.where(lane_lo, invs[0], invs[1])


def _lat_attention(sink, z3, ck2, cv2, cos_t, sin_t):
    bsz, t, _ = z3.shape
    nb = t // QBLK
    tc2 = ck2.shape[2]
    return pl.pallas_call(
        _lat_attn_kernel,
        grid=(bsz, nb),
        in_specs=[pl.BlockSpec(memory_space=pltpu.SMEM),
                  pl.BlockSpec((1, QBLK, D_MODEL), lambda b, i: (b, i, COL_Q // D_MODEL)),
                  pl.BlockSpec((1, t, KV_WIDTH), lambda b, i: (b, 0, COL_KB // KV_WIDTH)),
                  pl.BlockSpec((1, t, KV_WIDTH), lambda b, i: (b, 0, COL_VB // KV_WIDTH)),
                  pl.BlockSpec((1, KV_HEADS, tc2, LANES), lambda b, i: (b, 0, 0, 0)),
                  pl.BlockSpec((1, KV_HEADS, tc2, LANES), lambda b, i: (b, 0, 0, 0)),
                  pl.BlockSpec((t, LANES), lambda b, i: (0, 0)),
                  pl.BlockSpec((t, LANES), lambda b, i: (0, 0))],
        out_specs=pl.BlockSpec((1, QBLK, D_MODEL), lambda b, i: (b, i, 0)),
        out_shape=jax.ShapeDtypeStruct((bsz, t, D_MODEL), F32),
        scratch_shapes=[pltpu.VMEM((KV_HEADS, nb, 2 * QBLK, LANES), BF16)] * 2,
        compiler_params=_cparams("arbitrary", "arbitrary"),
        name="lat_attn",
    )(sink, z3, z3, z3, ck2, cv2, cos_t, sin_t)


def _merge_kernel(x_ref, mod_ref, ya_ref, yb_ref, ga_ref, gb_ref, ma_ref, mb_ref,
                  woa_ref, wob_ref, wout_ref, fw_ref, o_ref):
    ga = ga_ref[...]
    gb = gb_ref[...]
    br_a = jnp.dot((ya_ref[...] * (ga * _sigmoid(ga))).astype(BF16), woa_ref[...], preferred_element_type=F32)
    br_b = jnp.dot((yb_ref[...] * (gb * _sigmoid(gb))).astype(BF16), wob_ref[...], preferred_element_type=F32)
    merged = _sigmoid(ma_ref[...]) * br_a + _sigmoid(mb_ref[...]) * br_b
    out = x_ref[...] + mod_ref[0, 2:3, :] * jnp.dot(merged.astype(BF16), wout_ref[...],
                                                     preferred_element_type=F32)
    o_ref[...] = out * lax.rsqrt(jnp.mean(out * out, axis=-1, keepdims=True) + RMS_EPS) * fw_ref[...]


def _merge(x2, mod3, ya2, yb2, z2, woa, wob, wout, fw, mod_row, tm):
    m = x2.shape[0]

    def tok(col=0):
        return pl.BlockSpec((tm, D_MODEL), lambda i, col=col: (i, col // D_MODEL))

    def wspec():
        return pl.BlockSpec((D_MODEL, D_MODEL), lambda i: (0, 0))

    return pl.pallas_call(
        _merge_kernel,
        grid=(m // tm,),
        in_specs=[tok(), pl.BlockSpec((1, 3, D_MODEL), lambda i: (mod_row(i), 0, 0)),
                  tok(), tok(), tok(COL_GA), tok(COL_GB), tok(COL_MA), tok(COL_MB),
                  wspec(), wspec(), wspec(), pl.BlockSpec((1, D_MODEL), lambda i: (0, 0))],
        out_specs=tok(),
        out_shape=jax.ShapeDtypeStruct((m, D_MODEL), F32),
        compiler_params=_cparams("arbitrary"),
        name="merge",
    )(x2, mod3, ya2, yb2, z2, z2, z2, z2, woa, wob, wout, fw.reshape(1, D_MODEL))


def _rope_tables(t):
    n_rows = t // GRID_W
    row = jnp.repeat(jnp.arange(n_rows), GRID_W)
    col = jnp.tile(jnp.arange(GRID_W), n_rows)
    nf = HEAD_DIM // 4
    inv = ROPE_BASE ** (-jnp.arange(nf, dtype=F32) / nf)
    ang_r = row.astype(F32)[:, None] * inv[None, :]
    ang_c = col.astype(F32)[:, None] * inv[None, :]
    cos = jnp.concatenate([jnp.cos(ang_r), jnp.cos(ang_r), jnp.cos(ang_c), jnp.cos(ang_c)], axis=-1)
    sin = jnp.concatenate([-jnp.sin(ang_r), jnp.sin(ang_r), -jnp.sin(ang_c), jnp.sin(ang_c)], axis=-1)
    return jnp.tile(cos, (1, 2)), jnp.tile(sin, (1, 2))


def _pad_heads(x):
    zero = jnp.zeros_like(x)
    return jnp.concatenate([jnp.concatenate([x, zero], axis=-1),
                            jnp.concatenate([zero, x], axis=-1)], axis=-2).astype(BF16)


def kernel(x_prompt, x_sample, cache_k, cache_v, state_rwkv, c, c_ctx, norm_w, w_ada, b_ada, w_in, conv_a,
           w0, w_up, a0, a_up, k_k, k_a, r_k, ln_x_w, ln_x_b, w_oA, sink, w_oB, w_out, final_norm_w):
    depth = norm_w.shape[0]
    assert depth == 1
    bc, tc, _ = x_prompt.shape
    bd, td, _ = x_sample.shape
    l = 0
    a_cols = 3 * D_MODEL + 2 * LORA

    wi = w_in[l]
    o_ga = a_cols
    o_q = o_ga + D_MODEL
    o_kb = o_q + D_MODEL
    o_vb = o_kb + KV_WIDTH
    o_gb = o_vb + KV_WIDTH
    o_ma = o_gb + D_MODEL
    o_mb = o_ma + D_MODEL
    w_in_p = jnp.concatenate(
        [wi[:, 0:3 * D_MODEL], wi[:, o_ga:o_q], wi[:, o_q:o_kb], wi[:, o_gb:o_ma], wi[:, o_ma:o_mb],
         wi[:, o_mb:o_mb + D_MODEL], wi[:, o_kb:o_vb], wi[:, o_vb:o_gb], wi[:, 3 * D_MODEL:a_cols]],
        axis=1).astype(BF16)
    conv_rkv = conv_a[l][:, 0:3 * D_MODEL]
    conv_wa = conv_a[l][:, 3 * D_MODEL:a_cols]
    pch = jnp.concatenate(
        [k_k[l][None], k_a[l][None], r_k[l].reshape(1, D_MODEL), ln_x_w[l][None], ln_x_b[l][None],
         w0[l], a0[l], jnp.zeros((7, D_MODEL), F32)], axis=0)
    wup = w_up[l].reshape(2, LORA, N_PAIRS, LANES).transpose(0, 2, 1, 3)
    aup = a_up[l].reshape(2, LORA, N_PAIRS, LANES).transpose(0, 2, 1, 3)
    zl = jnp.zeros_like(wup)
    lora = jnp.concatenate([jnp.concatenate([wup, zl], axis=-1),
                            jnp.concatenate([zl, aup], axis=-1)], axis=-2).astype(BF16)
    woa, wob, wout = w_oA[l].astype(BF16), w_oB[l].astype(BF16), w_out[l].astype(BF16)
    sink_l = sink[l]

    cond8 = jnp.concatenate([c_ctx[None], c, jnp.zeros((8 - 1 - bd, D_MODEL), F32)], axis=0)
    mod3 = _modulation(cond8, w_ada[l], b_ada[l]).reshape(8, 3, D_MODEL)

    tm = 512
    ctx_row = lambda i: 0
    lat_row = lambda i: 1 + (i * tm) // td

    xp2 = x_prompt.reshape(bc * tc, D_MODEL)
    zc = _inproj(xp2, mod3, norm_w[l], w_in_p, ctx_row, tm)
    zc3 = zc.reshape(bc, tc, ZW)
    ya_c, st_c = _rwkv(zc3, conv_rkv, conv_wa, pch, lora, None)
    yb_c, new_k, new_v = _ctx_attention(sink_l, zc3)
    y_prompt = _merge(xp2, mod3, ya_c.reshape(bc * tc, D_MODEL), yb_c.reshape(bc * tc, D_MODEL), zc,
                      woa, wob, wout, final_norm_w, ctx_row, tm).reshape(bc, tc, D_MODEL)
    new_state = st_c.reshape(bc, 2, N_PAIRS, HEAD_DIM, 2, HEAD_DIM).transpose(0, 1, 2, 4, 3, 5)
    new_state = new_state.reshape(bc, 1, 2, 2 * N_PAIRS, HEAD_DIM, HEAD_DIM)

    xs2 = x_sample.reshape(bd * td, D_MODEL)
    zd = _inproj(xs2, mod3, norm_w[l], w_in_p, lat_row, tm)
    zd3 = zd.reshape(bd, td, ZW)
    s0p = state_rwkv[:, l].reshape(bd, 2, N_PAIRS, 2, HEAD_DIM, HEAD_DIM).transpose(0, 1, 2, 4, 3, 5)
    s0p = s0p.reshape(bd, 2, N_PAIRS, HEAD_DIM, LANES)
    ya_d, _ = _rwkv(zd3, conv_rkv, conv_wa, pch, lora, s0p)
    cos_t, sin_t = _rope_tables(td)
    yb_d = _lat_attention(sink_l, zd3, _pad_heads(cache_k[:, l]), _pad_heads(cache_v[:, l]), cos_t, sin_t)
    y_sample = _merge(xs2, mod3, ya_d.reshape(bd * td, D_MODEL), yb_d.reshape(bd * td, D_MODEL), zd,
                      woa, wob, wout, final_norm_w, lat_row, tm).reshape(bd, td, D_MODEL)

    return (y_prompt, y_sample, new_k[:, None], new_v[:, None], new_state)
```

```python
import functools

import jax
import jax.numpy as jnp
from jax import lax
from jax.experimental import pallas as pl
from jax.experimental.pallas import tpu as pltpu

F32 = jnp.float32
BF16 = jnp.bfloat16

D_MODEL = 1024
HEAD_DIM = 64
LANES = 128
N_PAIRS = D_MODEL // LANES
KV_HEADS = 4
KV_WIDTH = KV_HEADS * HEAD_DIM
LORA = 64
GRID_W = 64
WINDOW = 128
QBLK = 128
ROPE_BASE = 10000.0
RMS_EPS = 1e-6
GN_EPS = 64e-5
NEG_INF = -1e30
ATTN_SCALE = HEAD_DIM ** -0.5
PAIRS_PER_STEP = 2
GROUP = 4
CHUNK = 64

COL_R, COL_K, COL_V, COL_GA, COL_Q, COL_GB, COL_MA, COL_MB = (i * D_MODEL for i in range(8))
COL_KB = 8 * D_MODEL
COL_VB = COL_KB + KV_WIDTH
COL_WA = COL_VB + KV_WIDTH
ZW = COL_WA + 2 * LORA
TN_IN = 2944

VMEM_LIMIT = 56 * 1024 * 1024


def _cparams(*sem):
    return pltpu.CompilerParams(dimension_semantics=sem, vmem_limit_bytes=VMEM_LIMIT)


def _sigmoid(x):
    return jax.nn.sigmoid(x)


def _mm(a, b):
    return jnp.dot(a.astype(BF16), b.astype(BF16), preferred_element_type=F32)


def _mm_nt(a, b):
    return lax.dot_general(a.astype(BF16), b.astype(BF16), (((1,), (1,)), ((), ())),
                           preferred_element_type=F32)


def _mod_kernel(c_ref, w_ref, b_ref, o_ref):
    c = c_ref[...]
    o_ref[...] = jnp.dot(c * _sigmoid(c), w_ref[...], preferred_element_type=F32) + b_ref[...]


def _modulation(cond8, w_ada, b_ada):
    n = w_ada.shape[1]
    tn = D_MODEL
    return pl.pallas_call(
        _mod_kernel,
        grid=(n // tn,),
        in_specs=[pl.BlockSpec((8, D_MODEL), lambda j: (0, 0)),
                  pl.BlockSpec((D_MODEL, tn), lambda j: (0, j)),
                  pl.BlockSpec((1, tn), lambda j: (0, j))],
        out_specs=pl.BlockSpec((8, tn), lambda j: (0, j)),
        out_shape=jax.ShapeDtypeStruct((8, n), F32),
        compiler_params=_cparams("arbitrary"),
        name="mod",
    )(cond8, w_ada, b_ada.reshape(1, n))


def _inproj_kernel(x_ref, mod_ref, nw_ref, w_ref, o_ref):
    x = x_ref[...]
    y = x * lax.rsqrt(jnp.mean(x * x, axis=-1, keepdims=True) + RMS_EPS) * nw_ref[...]
    h = y * (1.0 + mod_ref[0, 1:2, :]) + mod_ref[0, 0:1, :]
    o_ref[...] = jnp.dot(h.astype(BF16), w_ref[...], preferred_element_type=F32)


def _inproj(x2, mod3, norm_w, w_in_p, mod_row, tm):
    m = x2.shape[0]
    return pl.pallas_call(
        _inproj_kernel,
        grid=(ZW // TN_IN, m // tm),
        in_specs=[pl.BlockSpec((tm, D_MODEL), lambda n, i: (i, 0)),
                  pl.BlockSpec((1, 3, D_MODEL), lambda n, i: (mod_row(i), 0, 0)),
                  pl.BlockSpec((1, D_MODEL), lambda n, i: (0, 0)),
                  pl.BlockSpec((D_MODEL, TN_IN), lambda n, i: (0, n))],
        out_specs=pl.BlockSpec((tm, TN_IN), lambda n, i: (i, n)),
        out_shape=jax.ShapeDtypeStruct((m, ZW), F32),
        compiler_params=_cparams("arbitrary", "arbitrary"),
        name="inproj",
    )(x2, mod3, norm_w.reshape(1, D_MODEL), w_in_p)


def _segsum(x, e_bd):
    return jnp.dot(x.astype(BF16), e_bd, preferred_element_type=F32)


def _bmm(a, b):
    return jnp.einsum("bij,bjk->bik", a.astype(BF16), b.astype(BF16), preferred_element_type=F32)


def _bmm_nt(a, b):
    return jnp.einsum("bik,bjk->bij", a.astype(BF16), b.astype(BF16), preferred_element_type=F32)


def _bmm_tn(a, b):
    return jnp.einsum("bki,bkj->bij", a.astype(BF16), b.astype(BF16), preferred_element_type=F32)


def _chunk_local(r, kd, v, a, b, lw):
    c = CHUNK
    n = r.shape[0]
    g = n // 2

    def by_dir(fwd, bwd):
        return jnp.concatenate([fwd, bwd], axis=0)

    def masked(mask_f, mask_r, x):
        return by_dir(jnp.where(mask_f, x[:g], 0.0), jnp.where(mask_r, x[g:], 0.0))

    ti = lax.broadcasted_iota(jnp.int32, (c, c), 0)
    tj = lax.broadcasted_iota(jnp.int32, (c, c), 1)
    tri = by_dir(jnp.broadcast_to(jnp.where(tj <= ti, 1.0, 0.0).astype(BF16), (g, c, c)),
                 jnp.broadcast_to(jnp.where(tj >= ti, 1.0, 0.0).astype(BF16), (g, c, c)))
    lw_hi = lw.astype(BF16)
    lw_lo = (lw - lw_hi.astype(F32)).astype(BF16)
    cs = _bmm(tri, lw_hi) + _bmm(tri, lw_lo)
    tot = by_dir(cs[:g, c - 1:c], cs[g:, 0:1])
    e_in = jnp.exp(cs)
    e_neg = jnp.exp(-cs)
    e_end = jnp.exp(tot - cs)
    at = a * jnp.exp(cs - lw)
    rt = r * e_in
    bt = b * e_neg
    kt = kd * e_neg
    bh = b * e_end
    kh = kd * e_end
    pc = jnp.exp(tot)

    h0 = lax.broadcasted_iota(jnp.int32, (c, LANES), 1) < HEAD_DIM

    def stack2(x):
        return jnp.concatenate([jnp.where(h0, x, 0.0), jnp.where(h0, 0.0, x)], axis=1)

    def collapse(x2):
        return x2[:, :c] + x2[:, c:]

    def twice(x):
        return jnp.concatenate([x, x], axis=1)

    gm = _bmm_nt(jnp.concatenate([at, rt], axis=1), jnp.concatenate([stack2(bt), stack2(kt)], axis=1))

    ri = lax.broadcasted_iota(jnp.int32, (LANES, LANES), 0)
    ci = lax.broadcasted_iota(jnp.int32, (LANES, LANES), 1)
    same = (ri >> 6) == (ci >> 6)
    rt_i = ri & (c - 1)
    ct_i = ci & (c - 1)
    strict_f, strict_r = same & (ct_i < rt_i), same & (ct_i > rt_i)
    incl_f, incl_r = same & (ct_i <= rt_i), same & (ct_i >= rt_i)
    diag = ri == ci

    a_ab = masked(strict_f, strict_r, twice(gm[:, 0:c, 0:LANES]))
    a_ak = masked(strict_f, strict_r, twice(gm[:, 0:c, LANES:])).astype(BF16)
    a_rb = masked(incl_f, incl_r, twice(gm[:, c:, 0:LANES])).astype(BF16)
    a_rk = masked(incl_f, incl_r, twice(gm[:, c:, LANES:])).astype(BF16)

    def blk(shift):
        return (ri >> shift) == (ci >> shift)

    eye = jnp.where(diag, 1.0, 0.0)
    a8 = jnp.where(blk(3), a_ab, 0.0).astype(BF16)
    a8_2 = _bmm(a8, a8)
    a8_4 = _bmm(a8_2, a8_2)
    tm = _bmm(_bmm(eye + a8, eye + a8_2), eye + a8_4)
    for sh in (3, 4, 5):
        off = jnp.where(blk(sh + 1) & jnp.logical_not(blk(sh)), a_ab, 0.0)
        tm16 = tm.astype(BF16)
        tm = tm + _bmm(_bmm(tm16, off), tm16)

    v2 = stack2(v).astype(BF16)
    at2 = stack2(at)
    x = _bmm(tm, jnp.concatenate([at2, _bmm(a_ak, v2)], axis=2))
    qy = _bmm(a_rb, x)
    qeff = rt + collapse(qy[:, :, :LANES])
    yloc = collapse(qy[:, :, LANES:] + _bmm(a_rk, v2))
    wu = collapse(x)
    md = _bmm_tn(jnp.concatenate([bh, kh], axis=1),
                 jnp.concatenate([wu, jnp.concatenate([jnp.zeros_like(v), v], axis=2)], axis=1))
    mt = jnp.where(same, md[:, :, :LANES], 0.0) + jnp.where(diag, pc, 0.0)
    dt = jnp.where(same, md[:, :, LANES:], 0.0)
    return qeff, yloc, mt, dt


def _rwkv_kernel(*refs, has_s0):
    if has_s0:
        (r_ref, k_ref, v_ref, wa_ref, cr_ref, ck_ref, cv_ref, cwa_ref, pch_ref, lora_ref, s0_ref,
         ya_ref, so_ref, r_s, v_s, a_s, bon_s, kd_s, b_s, lw_s, y_s, q_s, mt_s, dt_s, z_s) = refs
    else:
        (r_ref, k_ref, v_ref, wa_ref, cr_ref, ck_ref, cv_ref, cwa_ref, pch_ref, lora_ref,
         ya_ref, so_ref, r_s, v_s, a_s, bon_s, kd_s, b_s, lw_s, y_s, q_s, mt_s, dt_s, z_s) = refs
        s0_ref = None
    pw = PAIRS_PER_STEP
    t = r_ref.shape[1]
    nc = t // CHUNK
    row = lax.broadcasted_iota(jnp.int32, (t, LANES), 0)
    lane = lax.broadcasted_iota(jnp.int32, (t, LANES), 1)
    h0 = lane < HEAD_DIM
    ri = lax.broadcasted_iota(jnp.int32, (LANES, LANES), 0)
    ci = lax.broadcasted_iota(jnp.int32, (LANES, LANES), 1)
    e_bd = jnp.where((ri >> 6) == (ci >> 6), 1.0, 0.0).astype(BF16)

    def conv(x, cw):
        prev = jnp.where(row == 0, 0.0, pltpu.roll(x, 1, axis=0))
        nxt = jnp.where(row == t - 1, 0.0, pltpu.roll(x, t - 1, axis=0))
        return cw[0:1, :] * prev + cw[1:2, :] * x + cw[2:3, :] * nxt

    wa = conv(wa_ref[0], cwa_ref[...])
    lora_in = jnp.where(h0, jnp.tanh(wa), wa).astype(BF16)
    for p in range(pw):
        ls = slice(p * LANES, (p + 1) * LANES)
        pch = pch_ref[:, ls]
        r = conv(r_ref[0, :, ls], cr_ref[:, ls])
        k = conv(k_ref[0, :, ls], ck_ref[:, ls])
        v = conv(v_ref[0, :, ls], cv_ref[:, ls])
        k_k, k_a, r_k = pch[0:1], pch[1:2], pch[2:3]
        kk = k * k_k
        kk = kk * lax.rsqrt(jnp.maximum(_segsum(kk * kk, e_bd), 1e-12))
        r_s[p] = r
        v_s[p] = v
        a_s[p] = -kk
        bon_s[p] = _segsum(r * k * r_k, e_bd) * v
        for d in range(2):
            lo = jnp.dot(lora_in, lora_ref[d, p], preferred_element_type=F32)
            lw_s[p, d] = (-jnp.exp(F32(-0.5))) * _sigmoid(pch[5 + d:6 + d] + lo[:, :LANES])
            a_sig = _sigmoid(pch[7 + d:8 + d] + lo[:, LANES:])
            kd_s[p, d] = k * (1.0 + (a_sig - 1.0) * k_a)
            b_s[p, d] = kk * a_sig

    gsz = GROUP * CHUNK

    def local_body(g, carry):
        rows = pl.ds(pl.multiple_of(g * gsz, gsz), gsz)
        cidx = pl.ds(pl.multiple_of(g * GROUP, GROUP), GROUP)

        def shared(ref):
            x = jnp.concatenate([ref[p, rows, :].reshape(GROUP, CHUNK, LANES) for p in range(pw)], axis=0)
            return jnp.concatenate([x, x], axis=0)

        def per_dir(ref):
            return jnp.concatenate([ref[p, d, rows, :].reshape(GROUP, CHUNK, LANES)
                                    for d in range(2) for p in range(pw)], axis=0)

        qeff, yloc, mt, dt = _chunk_local(shared(r_s), per_dir(kd_s), shared(v_s), shared(a_s),
                                          per_dir(b_s), per_dir(lw_s))
        for d in range(2):
            for p in range(pw):
                part = slice((d * pw + p) * GROUP, (d * pw + p + 1) * GROUP)
                q_s[p, d, rows, :] = qeff[part].reshape(gsz, LANES)
                y_s[p, d, rows, :] = yloc[part].reshape(gsz, LANES)
                mt_s[p, d, cidx] = mt[part]
                dt_s[p, d, cidx] = dt[part]
        return carry

    lax.fori_loop(0, nc // GROUP, local_body, 0)

    hl = lax.broadcasted_iota(jnp.int32, (HEAD_DIM, LANES), 1) < HEAD_DIM
    for p in range(pw):
        for d in range(2):
            if s0_ref is None:
                z_s[p, d] = jnp.zeros((LANES, LANES), F32)
            else:
                sp = s0_ref[0, d, p]
                zt = jnp.concatenate([jnp.where(hl, sp, 0.0), jnp.where(hl, 0.0, sp)], axis=0)
                z_s[p, d] = zt.T

    def seq_body(i, carry):
        for p in range(pw):
            for d in range(2):
                c = i if d == 0 else nc - 1 - i
                sl = pl.ds(pl.multiple_of(c * CHUNK, CHUNK), CHUNK)
                zb = z_s[p, d].astype(BF16)
                y_s[p, d, sl, :] = y_s[p, d, sl, :] + jnp.dot(q_s[p, d, sl, :].astype(BF16), zb,
                                                              preferred_element_type=F32)
                z_s[p, d] = jnp.dot(mt_s[p, d, c].astype(BF16), zb, preferred_element_type=F32) + dt_s[p, d, c]
        return carry

    lax.fori_loop(0, nc, seq_body, 0)

    for p in range(pw):
        ls = slice(p * LANES, (p + 1) * LANES)
        for d in range(2):
            zt = z_s[p, d].T
            so_ref[0, d, p] = zt[:HEAD_DIM] + zt[HEAD_DIM:]
        y = y_s[p, 0] + y_s[p, 1]
        mu = _segsum(y, e_bd) * (1.0 / HEAD_DIM)
        yc = y - mu
        var = _segsum(yc * yc, e_bd) * (1.0 / HEAD_DIM)
        ya_ref[0, :, ls] = (yc * lax.rsqrt(var + GN_EPS) * pch_ref[3:4, ls] + pch_ref[4:5, ls] + bon_s[p])


def _rwkv(z3, conv_rkv, conv_wa, pch, lora, s0p):
    bsz, t, _ = z3.shape
    nc = t // CHUNK
    pw = PAIRS_PER_STEP
    wl = pw * LANES
    has_s0 = s0p is not None

    def zspec(col):
        return pl.BlockSpec((1, t, wl), lambda b, j, col=col: (b, 0, col // wl + j))

    def cspec(col):
        return pl.BlockSpec((3, wl), lambda b, j, col=col: (0, col // wl + j))

    in_specs = [zspec(COL_R), zspec(COL_K), zspec(COL_V),
                pl.BlockSpec((1, t, LANES), lambda b, j: (b, 0, COL_WA // LANES)),
                cspec(COL_R), cspec(COL_K), cspec(COL_V),
                pl.BlockSpec((3, LANES), lambda b, j: (0, 0)),
                pl.BlockSpec((16, wl), lambda b, j: (0, j)),
                pl.BlockSpec((2, pw, LANES, 2 * LANES), lambda b, j: (0, j, 0, 0))]
    args = [z3, z3, z3, z3, conv_rkv, conv_rkv, conv_rkv, conv_wa, pch, lora]
    if has_s0:
        in_specs.append(pl.BlockSpec((1, 2, pw, HEAD_DIM, LANES), lambda b, j: (b, 0, j, 0, 0)))
        args.append(s0p)
    tl = (t, LANES)
    scratch = [pltpu.VMEM((pw,) + tl, F32)] * 4 + [pltpu.VMEM((pw, 2) + tl, F32)] * 5 \
        + [pltpu.VMEM((pw, 2, nc, LANES, LANES), F32)] * 2 + [pltpu.VMEM((pw, 2, LANES, LANES), F32)]
    return pl.pallas_call(
        functools.partial(_rwkv_kernel, has_s0=has_s0),
        grid=(bsz, N_PAIRS // pw),
        in_specs=in_specs,
        out_specs=[pl.BlockSpec((1, t, wl), lambda b, j: (b, 0, j)),
                   pl.BlockSpec((1, 2, pw, HEAD_DIM, LANES), lambda b, j: (b, 0, j, 0, 0))],
        out_shape=[jax.ShapeDtypeStruct((bsz, t, D_MODEL), F32),
                   jax.ShapeDtypeStruct((bsz, 2, N_PAIRS, HEAD_DIM, LANES), F32)],
        scratch_shapes=scratch,
        compiler_params=_cparams("arbitrary", "arbitrary"),
        name="rwkv_s0" if has_s0 else "rwkv",
    )(*args)


def _split_heads(x, odd):
    lo = lax.broadcasted_iota(jnp.int32, x.shape, 1) < HEAD_DIM
    if odd:
        xb = jnp.where(lo, 0.0, x)
        return pltpu.roll(xb, HEAD_DIM, axis=1), xb
    xa = jnp.where(lo, x, 0.0)
    return xa, pltpu.roll(xa, HEAD_DIM, axis=1)


def _softmax_parts(parts, sink):
    m = jnp.maximum(functools.reduce(jnp.maximum, [jnp.max(p, axis=-1, keepdims=True) for p in parts]), sink)
    es = [jnp.exp(p - m) for p in parts]
    den = functools.reduce(jnp.add, [jnp.sum(e, axis=-1, keepdims=True) for e in es]) + jnp.exp(sink - m)
    return es, 1.0 / den


def _ctx_attn_kernel(sink_ref, q_ref, k_ref, v_ref, o_ref, nk_ref, nv_ref):
    t = q_ref.shape[1]
    k = k_ref[0]
    v = v_ref[0]
    lane_lo = lax.broadcasted_iota(jnp.int32, (t, LANES), 1) < HEAD_DIM
    for hk in range(KV_HEADS):
        nk_ref[0, hk] = k[:, hk * HEAD_DIM:(hk + 1) * HEAD_DIM]
        nv_ref[0, hk] = v[:, hk * HEAD_DIM:(hk + 1) * HEAD_DIM]
    for hk in range(KV_HEADS):
        cb = hk // 2
        ka, kb = _split_heads(k[:, cb * LANES:(cb + 1) * LANES], hk % 2)
        va, vb = _split_heads(v[:, cb * LANES:(cb + 1) * LANES], hk % 2)
        kab = jnp.concatenate([ka, kb], axis=0).astype(BF16)
        vab = jnp.concatenate([va, vb], axis=0).astype(BF16)
        for pp in range(2):
            p = 2 * hk + pp
            q2 = (q_ref[0, :, p * LANES:(p + 1) * LANES] * ATTN_SCALE).astype(BF16)
            s = _mm_nt(q2, kab)
            e0, i0 = _softmax_parts([s[:, :t]], sink_ref[hk, 2 * pp])
            e1, i1 = _softmax_parts([s[:, t:]], sink_ref[hk, 2 * pp + 1])
            o = _mm(jnp.concatenate([e0[0], e1[0]], axis=1), vab)
            o_ref[0, :, p * LANES:(p + 1) * LANES] = o * jnp.where(lane_lo, i0, i1)


def _ctx_attention(sink, z3):
    bsz, t, _ = z3.shape
    return pl.pallas_call(
        _ctx_attn_kernel,
        grid=(bsz,),
        in_specs=[pl.BlockSpec(memory_space=pltpu.SMEM),
                  pl.BlockSpec((1, t, D_MODEL), lambda b: (b, 0, COL_Q // D_MODEL)),
                  pl.BlockSpec((1, t, KV_WIDTH), lambda b: (b, 0, COL_KB // KV_WIDTH)),
                  pl.BlockSpec((1, t, KV_WIDTH), lambda b: (b, 0, COL_VB // KV_WIDTH))],
        out_specs=[pl.BlockSpec((1, t, D_MODEL), lambda b: (b, 0, 0)),
                   pl.BlockSpec((1, KV_HEADS, t, HEAD_DIM), lambda b: (b, 0, 0, 0)),
                   pl.BlockSpec((1, KV_HEADS, t, HEAD_DIM), lambda b: (b, 0, 0, 0))],
        out_shape=[jax.ShapeDtypeStruct((bsz, t, D_MODEL), F32),
                   jax.ShapeDtypeStruct((bsz, KV_HEADS, t, HEAD_DIM), F32),
                   jax.ShapeDtypeStruct((bsz, KV_HEADS, t, HEAD_DIM), F32)],
        compiler_params=_cparams("arbitrary"),
        name="ctx_attn",
    )(sink, z3, z3, z3)


def _rope(x, cos, sin_signed):
    lane = lax.broadcasted_iota(jnp.int32, x.shape, 1)
    partner = jnp.where((lane & 31) < 16, pltpu.roll(x, LANES - 16, axis=1), pltpu.roll(x, 16, axis=1))
    return x * cos + partner * sin_signed


def _lat_attn_kernel(sink_ref, q_ref, k_ref, v_ref, ck_ref, cv_ref, cos_ref, sin_ref, o_ref, kab_s, vab_s):
    t = k_ref.shape[1]
    nb = t // QBLK
    i = pl.program_id(1)

    @pl.when(i == 0)
    def _():
        cos = cos_ref[...]
        sin = sin_ref[...]
        for cb in range(KV_HEADS // 2):
            kr = _rope(k_ref[0, :, cb * LANES:(cb + 1) * LANES], cos, sin)
            vv = v_ref[0, :, cb * LANES:(cb + 1) * LANES]
            for odd in range(2):
                hk = 2 * cb + odd
                ka, kb = _split_heads(kr, odd)
                va, vb = _split_heads(vv, odd)
                for j in range(nb):
                    rows = slice(j * QBLK, (j + 1) * QBLK)
                    kab_s[hk, j, 0:QBLK, :] = ka[rows].astype(BF16)
                    kab_s[hk, j, QBLK:2 * QBLK, :] = kb[rows].astype(BF16)
                    vab_s[hk, j, 0:QBLK, :] = va[rows].astype(BF16)
                    vab_s[hk, j, QBLK:2 * QBLK, :] = vb[rows].astype(BF16)

    qrows = pl.ds(pl.multiple_of(i * QBLK, QBLK), QBLK)
    cos_q = cos_ref[qrows, :]
    sin_q = sin_ref[qrows, :]
    ri = lax.broadcasted_iota(jnp.int32, (QBLK, QBLK), 0)
    ci = lax.broadcasted_iota(jnp.int32, (QBLK, QBLK), 1)
    mask_prev = ci >= ri + jnp.where(i >= 1, 0, QBLK)
    mask_next = ci <= ri - jnp.where(i + 1 < nb, 0, QBLK)
    lane_lo = lax.broadcasted_iota(jnp.int32, (QBLK, LANES), 1) < HEAD_DIM
    jm = jnp.maximum(i - 1, 0)
    jp = jnp.minimum(i + 1, nb - 1)
    for hk in range(KV_HEADS):
        kblk = [kab_s[hk, jm], kab_s[hk, i], kab_s[hk, jp]]
        vblk = [vab_s[hk, jm], vab_s[hk, i], vab_s[hk, jp]]
        ckh = ck_ref[0, hk]
        cvh = cv_ref[0, hk]
        tc = ckh.shape[0] // 2
        for pp in range(2):
            p = 2 * hk + pp
            q2 = (_rope(q_ref[0, :, p * LANES:(p + 1) * LANES], cos_q, sin_q) * ATTN_SCALE).astype(BF16)
            s_loc = [_mm_nt(q2, kb) for kb in kblk]
            s_ctx = _mm_nt(q2, ckh)
            exps, invs = [], []
            for half in range(2):
                cols = slice(half * QBLK, (half + 1) * QBLK)
                parts = [jnp.where(mask_prev, s_loc[0][:, cols], NEG_INF),
                         s_loc[1][:, cols],
                         jnp.where(mask_next, s_loc[2][:, cols], NEG_INF),
                         s_ctx[:, half * tc:(half + 1) * tc]]
                es, inv = _softmax_parts(parts, sink_ref[hk, 2 * pp + half])
                exps.append(es)
                invs.append(inv)
            o = _mm(jnp.concatenate([exps[0][3], exps[1][3]], axis=1), cvh)
            for jj in range(3):
                o = o + _mm(jnp.concatenate([exps[0][jj], exps[1][jj]], axis=1), vblk[jj])
            o_ref[0, :, p * LANES:(p + 1) * LANES] = o * jnp.where(lane_lo, invs[0], invs[1])


def _lat_attention(sink, z3, ck2, cv2, cos_t, sin_t):
    bsz, t, _ = z3.shape
    nb = t // QBLK
    tc2 = ck2.shape[2]
    return pl.pallas_call(
        _lat_attn_kernel,
        grid=(bsz, nb),
        in_specs=[pl.BlockSpec(memory_space=pltpu.SMEM),
                  pl.BlockSpec((1, QBLK, D_MODEL), lambda b, i: (b, i, COL_Q // D_MODEL)),
                  pl.BlockSpec((1, t, KV_WIDTH), lambda b, i: (b, 0, COL_KB // KV_WIDTH)),
                  pl.BlockSpec((1, t, KV_WIDTH), lambda b, i: (b, 0, COL_VB // KV_WIDTH)),
                  pl.BlockSpec((1, KV_HEADS, tc2, LANES), lambda b, i: (b, 0, 0, 0)),
                  pl.BlockSpec((1, KV_HEADS, tc2, LANES), lambda b, i: (b, 0, 0, 0)),
                  pl.BlockSpec((t, LANES), lambda b, i: (0, 0)),
                  pl.BlockSpec((t, LANES), lambda b, i: (0, 0))],
        out_specs=pl.BlockSpec((1, QBLK, D_MODEL), lambda b, i: (b, i, 0)),
        out_shape=jax.ShapeDtypeStruct((bsz, t, D_MODEL), F32),
        scratch_shapes=[pltpu.VMEM((KV_HEADS, nb, 2 * QBLK, LANES), BF16)] * 2,
        compiler_params=_cparams("arbitrary", "arbitrary"),
        name="lat_attn",
    )(sink, z3, z3, z3, ck2, cv2, cos_t, sin_t)


def _merge_kernel(x_ref, mod_ref, ya_ref, yb_ref, ga_ref, gb_ref, ma_ref, mb_ref,
                  woa_ref, wob_ref, wout_ref, fw_ref, o_ref):
    ga = ga_ref[...]
    gb = gb_ref[...]
    br_a = jnp.dot((ya_ref[...] * (ga * _sigmoid(ga))).astype(BF16), woa_ref[...], preferred_element_type=F32)
    br_b = jnp.dot((yb_ref[...] * (gb * _sigmoid(gb))).astype(BF16), wob_ref[...], preferred_element_type=F32)
    merged = _sigmoid(ma_ref[...]) * br_a + _sigmoid(mb_ref[...]) * br_b
    out = x_ref[...] + mod_ref[0, 2:3, :] * jnp.dot(merged.astype(BF16), wout_ref[...],
                                                     preferred_element_type=F32)
    o_ref[...] = out * lax.rsqrt(jnp.mean(out * out, axis=-1, keepdims=True) + RMS_EPS) * fw_ref[...]


def _merge(x2, mod3, ya2, yb2, z2, woa, wob, wout, fw, mod_row, tm):
    m = x2.shape[0]

    def tok(col=0):
        return pl.BlockSpec((tm, D_MODEL), lambda i, col=col: (i, col // D_MODEL))

    def wspec():
        return pl.BlockSpec((D_MODEL, D_MODEL), lambda i: (0, 0))

    return pl.pallas_call(
        _merge_kernel,
        grid=(m // tm,),
        in_specs=[tok(), pl.BlockSpec((1, 3, D_MODEL), lambda i: (mod_row(i), 0, 0)),
                  tok(), tok(), tok(COL_GA), tok(COL_GB), tok(COL_MA), tok(COL_MB),
                  wspec(), wspec(), wspec(), pl.BlockSpec((1, D_MODEL), lambda i: (0, 0))],
        out_specs=tok(),
        out_shape=jax.ShapeDtypeStruct((m, D_MODEL), F32),
        compiler_params=_cparams("arbitrary"),
        name="merge",
    )(x2, mod3, ya2, yb2, z2, z2, z2, z2, woa, wob, wout, fw.reshape(1, D_MODEL))


def _rope_tables(t):
    n_rows = t // GRID_W
    row = jnp.repeat(jnp.arange(n_rows), GRID_W)
    col = jnp.tile(jnp.arange(GRID_W), n_rows)
    nf = HEAD_DIM // 4
    inv = ROPE_BASE ** (-jnp.arange(nf, dtype=F32) / nf)
    ang_r = row.astype(F32)[:, None] * inv[None, :]
    ang_c = col.astype(F32)[:, None] * inv[None, :]
    cos = jnp.concatenate([jnp.cos(ang_r), jnp.cos(ang_r), jnp.cos(ang_c), jnp.cos(ang_c)], axis=-1)
    sin = jnp.concatenate([-jnp.sin(ang_r), jnp.sin(ang_r), -jnp.sin(ang_c), jnp.sin(ang_c)], axis=-1)
    return jnp.tile(cos, (1, 2)), jnp.tile(sin, (1, 2))


def _pad_heads(x):
    zero = jnp.zeros_like(x)
    return jnp.concatenate([jnp.concatenate([x, zero], axis=-1),
                            jnp.concatenate([zero, x], axis=-1)], axis=-2).astype(BF16)


def kernel(x_prompt, x_sample, cache_k, cache_v, state_rwkv, c, c_ctx, norm_w, w_ada, b_ada, w_in, conv_a,
           w0, w_up, a0, a_up, k_k, k_a, r_k, ln_x_w, ln_x_b, w_oA, sink, w_oB, w_out, final_norm_w):
    depth = norm_w.shape[0]
    assert depth == 1
    bc, tc, _ = x_prompt.shape
    bd, td, _ = x_sample.shape
    l = 0
    a_cols = 3 * D_MODEL + 2 * LORA

    wi = w_in[l]
    o_ga = a_cols
    o_q = o_ga + D_MODEL
    o_kb = o_q + D_MODEL
    o_vb = o_kb + KV_WIDTH
    o_gb = o_vb + KV_WIDTH
    o_ma = o_gb + D_MODEL
    o_mb = o_ma + D_MODEL
    w_in_p = jnp.concatenate(
        [wi[:, 0:3 * D_MODEL], wi[:, o_ga:o_q], wi[:, o_q:o_kb], wi[:, o_gb:o_ma], wi[:, o_ma:o_mb],
         wi[:, o_mb:o_mb + D_MODEL], wi[:, o_kb:o_vb], wi[:, o_vb:o_gb], wi[:, 3 * D_MODEL:a_cols]],
        axis=1).astype(BF16)
    conv_rkv = conv_a[l][:, 0:3 * D_MODEL]
    conv_wa = conv_a[l][:, 3 * D_MODEL:a_cols]
    pch = jnp.concatenate(
        [k_k[l][None], k_a[l][None], r_k[l].reshape(1, D_MODEL), ln_x_w[l][None], ln_x_b[l][None],
         w0[l], a0[l], jnp.zeros((7, D_MODEL), F32)], axis=0)
    wup = w_up[l].reshape(2, LORA, N_PAIRS, LANES).transpose(0, 2, 1, 3)
    aup = a_up[l].reshape(2, LORA, N_PAIRS, LANES).transpose(0, 2, 1, 3)
    zl = jnp.zeros_like(wup)
    lora = jnp.concatenate([jnp.concatenate([wup, zl], axis=-1),
                            jnp.concatenate([zl, aup], axis=-1)], axis=-2).astype(BF16)
    woa, wob, wout = w_oA[l].astype(BF16), w_oB[l].astype(BF16), w_out[l].astype(BF16)
    sink_l = sink[l]

    cond8 = jnp.concatenate([c_ctx[None], c, jnp.zeros((8 - 1 - bd, D_MODEL), F32)], axis=0)
    mod3 = _modulation(cond8, w_ada[l], b_ada[l]).reshape(8, 3, D_MODEL)

    tm = 512
    ctx_row = lambda i: 0
    lat_row = lambda i: 1 + (i * tm) // td

    xp2 = x_prompt.reshape(bc * tc, D_MODEL)
    zc = _inproj(xp2, mod3, norm_w[l], w_in_p, ctx_row, tm)
    zc3 = zc.reshape(bc, tc, ZW)
    ya_c, st_c = _rwkv(zc3, conv_rkv, conv_wa, pch, lora, None)
    yb_c, new_k, new_v = _ctx_attention(sink_l, zc3)
    y_prompt = _merge(xp2, mod3, ya_c.reshape(bc * tc, D_MODEL), yb_c.reshape(bc * tc, D_MODEL), zc,
                      woa, wob, wout, final_norm_w, ctx_row, tm).reshape(bc, tc, D_MODEL)
    new_state = st_c.reshape(bc, 2, N_PAIRS, HEAD_DIM, 2, HEAD_DIM).transpose(0, 1, 2, 4, 3, 5)
    new_state = new_state.reshape(bc, 1, 2, 2 * N_PAIRS, HEAD_DIM, HEAD_DIM)

    xs2 = x_sample.reshape(bd * td, D_MODEL)
    zd = _inproj(xs2, mod3, norm_w[l], w_in_p, lat_row, tm)
    zd3 = zd.reshape(bd, td, ZW)
    s0p = state_rwkv[:, l].reshape(bd, 2, N_PAIRS, 2, HEAD_DIM, HEAD_DIM).transpose(0, 1, 2, 4, 3, 5)
    s0p = s0p.reshape(bd, 2, N_PAIRS, HEAD_DIM, LANES)
    ya_d, _ = _rwkv(zd3, conv_rkv, conv_wa, pch, lora, s0p)
    cos_t, sin_t = _rope_tables(td)
    yb_d = _lat_attention(sink_l, zd3, _pad_heads(cache_k[:, l]), _pad_heads(cache_v[:, l]), cos_t, sin_t)
    y_sample = _merge(xs2, mod3, ya_d.reshape(bd * td, D_MODEL), yb_d.reshape(bd * td, D_MODEL), zd,
                      woa, wob, wout, final_norm_w, lat_row, tm).reshape(bd, td, D_MODEL)

    return (y_prompt, y_sample, new_k[:, None], new_v[:, None], new_state)
```

```python
import functools
import math

import jax
import jax.numpy as jnp
from jax import lax
from jax.experimental import pallas as pl
from jax.experimental.pallas import tpu as pltpu

F32 = jnp.float32
BF16 = jnp.bfloat16

D_MODEL = 1024
HEAD_DIM = 64
LANES = 128
N_PAIRS = D_MODEL // LANES
KV_HEADS = 4
KV_WIDTH = KV_HEADS * HEAD_DIM
LORA = 64
GRID_W = 64
WINDOW = 128
QBLK = 128
ROPE_BASE = 10000.0
RMS_EPS = 1e-6
GN_EPS = 64e-5
NEG_INF = -1e30
ATTN_SCALE = HEAD_DIM ** -0.5
PAIRS_CTX, PAIRS_LAT = 4, 2
GROUP = 4
HALF_DECAY_SCALE = -0.5 * math.exp(-0.5)
CHUNK = 64

COL_R, COL_K, COL_V, COL_GA, COL_Q, COL_GB, COL_MA, COL_MB = (i * D_MODEL for i in range(8))
COL_KB = 8 * D_MODEL
COL_VB = COL_KB + KV_WIDTH
COL_WA = COL_VB + KV_WIDTH
ZW = COL_WA + 2 * LORA
TN_IN = 2944

VMEM_LIMIT = 56 * 1024 * 1024


def _cparams(*sem):
    return pltpu.CompilerParams(dimension_semantics=sem, vmem_limit_bytes=VMEM_LIMIT)


def _sigmoid(x):
    return 0.5 * jnp.tanh(0.5 * x) + 0.5


def _mm(a, b):
    return jnp.dot(a.astype(BF16), b.astype(BF16), preferred_element_type=F32)


def _mm_nt(a, b):
    return lax.dot_general(a.astype(BF16), b.astype(BF16), (((1,), (1,)), ((), ())),
                           preferred_element_type=F32)


def _mod_kernel(c_ref, w_ref, b_ref, o_ref):
    c = c_ref[...]
    o_ref[...] = jnp.dot(c * _sigmoid(c), w_ref[...], preferred_element_type=F32) + b_ref[...]


def _modulation(cond8, w_ada, b_ada):
    n = w_ada.shape[1]
    tn = D_MODEL
    return pl.pallas_call(
        _mod_kernel,
        grid=(n // tn,),
        in_specs=[pl.BlockSpec((8, D_MODEL), lambda j: (0, 0)),
                  pl.BlockSpec((D_MODEL, tn), lambda j: (0, j)),
                  pl.BlockSpec((1, tn), lambda j: (0, j))],
        out_specs=pl.BlockSpec((8, tn), lambda j: (0, j)),
        out_shape=jax.ShapeDtypeStruct((8, n), F32),
        compiler_params=_cparams("arbitrary"),
        name="mod",
    )(cond8, w_ada, b_ada.reshape(1, n))


def _inproj_kernel(x_ref, mod_ref, nw_ref, w_ref, o_ref):
    x = x_ref[...]
    y = x * lax.rsqrt(jnp.mean(x * x, axis=-1, keepdims=True) + RMS_EPS) * nw_ref[...]
    h = y * (1.0 + mod_ref[0, 1:2, :]) + mod_ref[0, 0:1, :]
    o_ref[...] = jnp.dot(h.astype(BF16), w_ref[...], preferred_element_type=F32)


def _inproj(x2, mod3, norm_w, w_in_p, mod_row, tm):
    m = x2.shape[0]
    return pl.pallas_call(
        _inproj_kernel,
        grid=(ZW // TN_IN, m // tm),
        in_specs=[pl.BlockSpec((tm, D_MODEL), lambda n, i: (i, 0)),
                  pl.BlockSpec((1, 3, D_MODEL), lambda n, i: (mod_row(i), 0, 0)),
                  pl.BlockSpec((1, D_MODEL), lambda n, i: (0, 0)),
                  pl.BlockSpec((D_MODEL, TN_IN), lambda n, i: (0, n))],
        out_specs=pl.BlockSpec((tm, TN_IN), lambda n, i: (i, n)),
        out_shape=jax.ShapeDtypeStruct((m, ZW), F32),
        compiler_params=_cparams("arbitrary", "arbitrary"),
        name="inproj",
    )(x2, mod3, norm_w.reshape(1, D_MODEL), w_in_p)


def _segsum(x, e_bd):
    return jnp.dot(x.astype(BF16), e_bd, preferred_element_type=F32)


def _bmm(a, b):
    return jnp.einsum("bij,bjk->bik", a.astype(BF16), b.astype(BF16), preferred_element_type=F32)


def _bmm_nt(a, b):
    return jnp.einsum("bik,bjk->bij", a.astype(BF16), b.astype(BF16), preferred_element_type=F32)


def _bmm_tn(a, b):
    return jnp.einsum("bki,bkj->bij", a.astype(BF16), b.astype(BF16), preferred_element_type=F32)


def _chunk_local(r, kd, v, a, b, lw):
    c = CHUNK
    n = r.shape[0]
    g = n // 2

    def by_dir(fwd, bwd):
        return jnp.concatenate([fwd, bwd], axis=0)

    def masked(mask_f, mask_r, x):
        return by_dir(jnp.where(mask_f, x[:g], 0.0), jnp.where(mask_r, x[g:], 0.0))

    ti = lax.broadcasted_iota(jnp.int32, (c, c), 0)
    tj = lax.broadcasted_iota(jnp.int32, (c, c), 1)
    tri = by_dir(jnp.broadcast_to(jnp.where(tj <= ti, 1.0, 0.0).astype(BF16), (g, c, c)),
                 jnp.broadcast_to(jnp.where(tj >= ti, 1.0, 0.0).astype(BF16), (g, c, c)))
    lw_hi = lw.astype(BF16)
    lw_lo = (lw - lw_hi.astype(F32)).astype(BF16)
    cs = _bmm(tri, lw_hi) + _bmm(tri, lw_lo)
    tot = by_dir(cs[:g, c - 1:c], cs[g:, 0:1])
    e_in = jnp.exp(cs)
    e_neg = jnp.exp(-cs)
    e_end = jnp.exp(tot - cs)
    at = a * jnp.exp(cs - lw)
    rt = r * e_in
    bt = b * e_neg
    kt = kd * e_neg
    bh = b * e_end
    kh = kd * e_end
    pc = jnp.exp(tot)

    h0 = lax.broadcasted_iota(jnp.int32, (c, LANES), 1) < HEAD_DIM

    def stack2(x):
        return jnp.concatenate([jnp.where(h0, x, 0.0), jnp.where(h0, 0.0, x)], axis=1)

    def collapse(x2):
        return x2[:, :c] + x2[:, c:]

    def twice(x):
        return jnp.concatenate([x, x], axis=1)

    gm = _bmm_nt(jnp.concatenate([at, rt], axis=1), jnp.concatenate([stack2(bt), stack2(kt)], axis=1))

    ri = lax.broadcasted_iota(jnp.int32, (LANES, LANES), 0)
    ci = lax.broadcasted_iota(jnp.int32, (LANES, LANES), 1)
    same = (ri >> 6) == (ci >> 6)
    rt_i = ri & (c - 1)
    ct_i = ci & (c - 1)
    strict_f, strict_r = same & (ct_i < rt_i), same & (ct_i > rt_i)
    incl_f, incl_r = same & (ct_i <= rt_i), same & (ct_i >= rt_i)
    diag = ri == ci

    a_ab = masked(strict_f, strict_r, twice(gm[:, 0:c, 0:LANES]))
    a_ak = masked(strict_f, strict_r, twice(gm[:, 0:c, LANES:])).astype(BF16)
    a_rb = masked(incl_f, incl_r, twice(gm[:, c:, 0:LANES])).astype(BF16)
    a_rk = masked(incl_f, incl_r, twice(gm[:, c:, LANES:])).astype(BF16)

    def blk(shift):
        return (ri >> shift) == (ci >> shift)

    eye = jnp.where(diag, 1.0, 0.0)
    a8 = jnp.where(blk(3), a_ab, 0.0).astype(BF16)
    a8_2 = _bmm(a8, a8)
    a8_4 = _bmm(a8_2, a8_2)
    tm = _bmm(_bmm(eye + a8, eye + a8_2), eye + a8_4)
    for sh in (3, 4, 5):
        off = jnp.where(blk(sh + 1) & jnp.logical_not(blk(sh)), a_ab, 0.0)
        tm16 = tm.astype(BF16)
        tm = tm + _bmm(_bmm(tm16, off), tm16)

    v2 = stack2(v).astype(BF16)
    at2 = stack2(at)
    x = _bmm(tm, jnp.concatenate([at2, _bmm(a_ak, v2)], axis=2))
    qy = _bmm(a_rb, x)
    qeff = rt + collapse(qy[:, :, :LANES])
    yloc = collapse(qy[:, :, LANES:] + _bmm(a_rk, v2))
    wu = collapse(x)
    md = _bmm_tn(jnp.concatenate([bh, kh], axis=1),
                 jnp.concatenate([wu, jnp.concatenate([jnp.zeros_like(v), v], axis=2)], axis=1))
    mt = jnp.where(same, md[:, :, :LANES], 0.0) + jnp.where(diag, pc, 0.0)
    dt = jnp.where(same, md[:, :, LANES:], 0.0)
    return qeff, yloc, mt, dt


def _rwkv_kernel(*refs, has_s0, pw):
    if has_s0:
        (r_ref, k_ref, v_ref, wa_ref, cr_ref, ck_ref, cv_ref, cwa_ref, pch_ref, lora_ref, s0_ref,
         ya_ref, so_ref, r_s, v_s, a_s, bon_s, kd_s, b_s, lw_s, y_s, q_s, mt_s, dt_s, z_s) = refs
    else:
        (r_ref, k_ref, v_ref, wa_ref, cr_ref, ck_ref, cv_ref, cwa_ref, pch_ref, lora_ref,
         ya_ref, so_ref, r_s, v_s, a_s, bon_s, kd_s, b_s, lw_s, y_s, q_s, mt_s, dt_s, z_s) = refs
        s0_ref = None
    t = r_ref.shape[1]
    nc = t // CHUNK
    h0 = lax.broadcasted_iota(jnp.int32, (t, LANES), 1) < HEAD_DIM
    ri = lax.broadcasted_iota(jnp.int32, (LANES, LANES), 0)
    ci = lax.broadcasted_iota(jnp.int32, (LANES, LANES), 1)
    e_bd = jnp.where((ri >> 6) == (ci >> 6), 1.0, 0.0).astype(BF16)

    row8 = lax.broadcasted_iota(jnp.int32, (8, LANES), 0)

    def conv(x, cw):
        prev = pltpu.roll(x, 1, axis=0)
        nxt = pltpu.roll(x, t - 1, axis=0)

        def taps(p, c, n):
            return cw[0:1, :] * p + cw[1:2, :] * c + cw[2:3, :] * n

        first = taps(jnp.where(row8 == 0, 0.0, prev[0:8]), x[0:8], nxt[0:8])
        last = taps(prev[t - 8:t], x[t - 8:t], jnp.where(row8 == 7, 0.0, nxt[t - 8:t]))
        mid = taps(prev[8:t - 8], x[8:t - 8], nxt[8:t - 8])
        return jnp.concatenate([first, mid, last], axis=0)

    wa = conv(wa_ref[0], cwa_ref[...])
    lora_in = jnp.where(h0, jnp.tanh(wa), wa).astype(BF16)
    for p in range(pw):
        ls = slice(p * LANES, (p + 1) * LANES)
        pch = pch_ref[:, ls]
        r = conv(r_ref[0, :, ls], cr_ref[:, ls])
        k = conv(k_ref[0, :, ls], ck_ref[:, ls])
        v = conv(v_ref[0, :, ls], cv_ref[:, ls])
        k_k, k_a, r_k = pch[0:1], pch[1:2], pch[2:3]
        kk = k * k_k
        kk = kk * lax.rsqrt(jnp.maximum(_segsum(kk * kk, e_bd), 1e-12))
        r_s[p] = r
        v_s[p] = v
        a_s[p] = -kk
        bon_s[p] = _segsum(r * k * r_k, e_bd) * v
        half_kk = 0.5 * kk
        for d in range(2):
            lo = jnp.dot(lora_in, lora_ref[d, p], preferred_element_type=F32)
            th_w = jnp.tanh(pch[5 + d:6 + d] + lo[:, :LANES])
            th_a = jnp.tanh(pch[7 + d:8 + d] + lo[:, LANES:])
            lw_s[p, d] = HALF_DECAY_SCALE * th_w + HALF_DECAY_SCALE
            kd_s[p, d] = k * ((1.0 - 0.5 * k_a) + (0.5 * k_a) * th_a)
            b_s[p, d] = half_kk * th_a + half_kk

    gsz = GROUP * CHUNK

    def local_body(g, carry):
        rows = pl.ds(pl.multiple_of(g * gsz, gsz), gsz)
        cidx = pl.ds(pl.multiple_of(g * GROUP, GROUP), GROUP)

        def shared(ref):
            x = jnp.concatenate([ref[p, rows, :].reshape(GROUP, CHUNK, LANES) for p in range(pw)], axis=0)
            return jnp.concatenate([x, x], axis=0)

        def per_dir(ref):
            return jnp.concatenate([ref[p, d, rows, :].reshape(GROUP, CHUNK, LANES)
                                    for d in range(2) for p in range(pw)], axis=0)

        qeff, yloc, mt, dt = _chunk_local(shared(r_s), per_dir(kd_s), shared(v_s), shared(a_s),
                                          per_dir(b_s), per_dir(lw_s))
        for d in range(2):
            for p in range(pw):
                part = slice((d * pw + p) * GROUP, (d * pw + p + 1) * GROUP)
                q_s[p, d, rows, :] = qeff[part].reshape(gsz, LANES)
                y_s[p, d, rows, :] = yloc[part].reshape(gsz, LANES)
                mt_s[p, d, cidx] = mt[part]
                dt_s[p, d, cidx] = dt[part]
        return carry

    lax.fori_loop(0, nc // GROUP, local_body, 0)

    hl = lax.broadcasted_iota(jnp.int32, (HEAD_DIM, LANES), 1) < HEAD_DIM
    for p in range(pw):
        for d in range(2):
            if s0_ref is None:
                z_s[p, d] = jnp.zeros((LANES, LANES), F32)
            else:
                sp = s0_ref[0, d, p]
                zt = jnp.concatenate([jnp.where(hl, sp, 0.0), jnp.where(hl, 0.0, sp)], axis=0)
                z_s[p, d] = zt.T

    def seq_body(i, carry):
        for p in range(pw):
            for d in range(2):
                c = i if d == 0 else nc - 1 - i
                sl = pl.ds(pl.multiple_of(c * CHUNK, CHUNK), CHUNK)
                zb = z_s[p, d].astype(BF16)
                y_s[p, d, sl, :] = y_s[p, d, sl, :] + jnp.dot(q_s[p, d, sl, :].astype(BF16), zb,
                                                              preferred_element_type=F32)
                z_s[p, d] = jnp.dot(mt_s[p, d, c].astype(BF16), zb, preferred_element_type=F32) + dt_s[p, d, c]
        return carry

    lax.fori_loop(0, nc, seq_body, 0)

    for p in range(pw):
        ls = slice(p * LANES, (p + 1) * LANES)
        for d in range(2):
            zt = z_s[p, d].T
            so_ref[0, d, 2 * p] = zt[:HEAD_DIM, :HEAD_DIM]
            so_ref[0, d, 2 * p + 1] = zt[HEAD_DIM:, HEAD_DIM:]
        y = y_s[p, 0] + y_s[p, 1]
        mu = _segsum(y, e_bd) * (1.0 / HEAD_DIM)
        yc = y - mu
        var = _segsum(yc * yc, e_bd) * (1.0 / HEAD_DIM)
        ya_ref[0, :, ls] = (yc * lax.rsqrt(var + GN_EPS) * pch_ref[3:4, ls] + pch_ref[4:5, ls] + bon_s[p])


def _rwkv(z3, conv_rkv, conv_wa, pch, lora, s0p, pw):
    bsz, t, _ = z3.shape
    nc = t // CHUNK
    wl = pw * LANES
    has_s0 = s0p is not None

    def zspec(col):
        return pl.BlockSpec((1, t, wl), lambda b, j, col=col: (b, 0, col // wl + j))

    def cspec(col):
        return pl.BlockSpec((3, wl), lambda b, j, col=col: (0, col // wl + j))

    in_specs = [zspec(COL_R), zspec(COL_K), zspec(COL_V),
                pl.BlockSpec((1, t, LANES), lambda b, j: (b, 0, COL_WA // LANES)),
                cspec(COL_R), cspec(COL_K), cspec(COL_V),
                pl.BlockSpec((3, LANES), lambda b, j: (0, 0)),
                pl.BlockSpec((16, wl), lambda b, j: (0, j)),
                pl.BlockSpec((2, pw, LANES, 2 * LANES), lambda b, j: (0, j, 0, 0))]
    args = [z3, z3, z3, z3, conv_rkv, conv_rkv, conv_rkv, conv_wa, pch, lora]
    if has_s0:
        in_specs.append(pl.BlockSpec((1, 2, pw, HEAD_DIM, LANES), lambda b, j: (b, 0, j, 0, 0)))
        args.append(s0p)
    tl = (t, LANES)
    scratch = [pltpu.VMEM((pw,) + tl, F32)] * 4 + [pltpu.VMEM((pw, 2) + tl, F32)] * 5 \
        + [pltpu.VMEM((pw, 2, nc, LANES, LANES), F32)] * 2 + [pltpu.VMEM((pw, 2, LANES, LANES), F32)]
    return pl.pallas_call(
        functools.partial(_rwkv_kernel, has_s0=has_s0, pw=pw),
        grid=(bsz, N_PAIRS // pw),
        in_specs=in_specs,
        out_specs=[pl.BlockSpec((1, t, wl), lambda b, j: (b, 0, j)),
                   pl.BlockSpec((1, 2, 2 * pw, HEAD_DIM, HEAD_DIM), lambda b, j: (b, 0, j, 0, 0))],
        out_shape=[jax.ShapeDtypeStruct((bsz, t, D_MODEL), F32),
                   jax.ShapeDtypeStruct((bsz, 2, 2 * N_PAIRS, HEAD_DIM, HEAD_DIM), F32)],
        scratch_shapes=scratch,
        compiler_params=_cparams("arbitrary", "arbitrary"),
        name="rwkv_s0" if has_s0 else "rwkv",
    )(*args)


def _split_heads(x, odd):
    lo = lax.broadcasted_iota(jnp.int32, x.shape, 1) < HEAD_DIM
    if odd:
        xb = jnp.where(lo, 0.0, x)
        return pltpu.roll(xb, HEAD_DIM, axis=1), xb
    xa = jnp.where(lo, x, 0.0)
    return xa, pltpu.roll(xa, HEAD_DIM, axis=1)


def _softmax_parts(parts, sink):
    m = jnp.maximum(functools.reduce(jnp.maximum, [jnp.max(p, axis=-1, keepdims=True) for p in parts]), sink)
    es = [jnp.exp(p - m) for p in parts]
    den = functools.reduce(jnp.add, [jnp.sum(e, axis=-1, keepdims=True) for e in es]) + jnp.exp(sink - m)
    return es, 1.0 / den


def _ctx_attn_kernel(sink_ref, q_ref, k_ref, v_ref, o_ref, nk_ref, nv_ref):
    t = q_ref.shape[1]
    k = k_ref[0]
    v = v_ref[0]
    lane_lo = lax.broadcasted_iota(jnp.int32, (t, LANES), 1) < HEAD_DIM
    for hk in range(KV_HEADS):
        nk_ref[0, hk] = k[:, hk * HEAD_DIM:(hk + 1) * HEAD_DIM]
        nv_ref[0, hk] = v[:, hk * HEAD_DIM:(hk + 1) * HEAD_DIM]
    for hk in range(KV_HEADS):
        cb = hk // 2
        ka, kb = _split_heads(k[:, cb * LANES:(cb + 1) * LANES], hk % 2)
        va, vb = _split_heads(v[:, cb * LANES:(cb + 1) * LANES], hk % 2)
        kab = jnp.concatenate([ka, kb], axis=0).astype(BF16)
        vab = jnp.concatenate([va, vb], axis=0).astype(BF16)
        for pp in range(2):
            p = 2 * hk + pp
            q2 = (q_ref[0, :, p * LANES:(p + 1) * LANES] * ATTN_SCALE).astype(BF16)
            s = _mm_nt(q2, kab)
            e0, i0 = _softmax_parts([s[:, :t]], sink_ref[hk, 2 * pp])
            e1, i1 = _softmax_parts([s[:, t:]], sink_ref[hk, 2 * pp + 1])
            o = _mm(jnp.concatenate([e0[0], e1[0]], axis=1), vab)
            o_ref[0, :, p * LANES:(p + 1) * LANES] = o * jnp.where(lane_lo, i0, i1)


def _ctx_attention(sink, z3):
    bsz, t, _ = z3.shape
    return pl.pallas_call(
        _ctx_attn_kernel,
        grid=(bsz,),
        in_specs=[pl.BlockSpec(memory_space=pltpu.SMEM),
                  pl.BlockSpec((1, t, D_MODEL), lambda b: (b, 0, COL_Q // D_MODEL)),
                  pl.BlockSpec((1, t, KV_WIDTH), lambda b: (b, 0, COL_KB // KV_WIDTH)),
                  pl.BlockSpec((1, t, KV_WIDTH), lambda b: (b, 0, COL_VB // KV_WIDTH))],
        out_specs=[pl.BlockSpec((1, t, D_MODEL), lambda b: (b, 0, 0)),
                   pl.BlockSpec((1, KV_HEADS, t, HEAD_DIM), lambda b: (b, 0, 0, 0)),
                   pl.BlockSpec((1, KV_HEADS, t, HEAD_DIM), lambda b: (b, 0, 0, 0))],
        out_shape=[jax.ShapeDtypeStruct((bsz, t, D_MODEL), F32),
                   jax.ShapeDtypeStruct((bsz, KV_HEADS, t, HEAD_DIM), F32),
                   jax.ShapeDtypeStruct((bsz, KV_HEADS, t, HEAD_DIM), F32)],
        compiler_params=_cparams("arbitrary"),
        name="ctx_attn",
    )(sink, z3, z3, z3)


def _rope(x, cos, sin_signed):
    lane = lax.broadcasted_iota(jnp.int32, x.shape, 1)
    partner = jnp.where((lane & 31) < 16, pltpu.roll(x, LANES - 16, axis=1), pltpu.roll(x, 16, axis=1))
    return x * cos + partner * sin_signed


def _lat_attn_kernel(sink_ref, q_ref, k_ref, v_ref, ck_ref, cv_ref, cos_ref, sin_ref, o_ref, kab_s, vab_s):
    t = k_ref.shape[1]
    nb = t // QBLK
    i = pl.program_id(1)

    @pl.when(i == 0)
    def _():
        cos = cos_ref[...]
        sin = sin_ref[...]
        for cb in range(KV_HEADS // 2):
            kr = _rope(k_ref[0, :, cb * LANES:(cb + 1) * LANES], cos, sin)
            vv = v_ref[0, :, cb * LANES:(cb + 1) * LANES]
            for odd in range(2):
                hk = 2 * cb + odd
                ka, kb = _split_heads(kr, odd)
                va, vb = _split_heads(vv, odd)
                for j in range(nb):
                    rows = slice(j * QBLK, (j + 1) * QBLK)
                    kab_s[hk, j, 0:QBLK, :] = ka[rows].astype(BF16)
                    kab_s[hk, j, QBLK:2 * QBLK, :] = kb[rows].astype(BF16)
                    vab_s[hk, j, 0:QBLK, :] = va[rows].astype(BF16)
                    vab_s[hk, j, QBLK:2 * QBLK, :] = vb[rows].astype(BF16)

    qrows = pl.ds(pl.multiple_of(i * QBLK, QBLK), QBLK)
    cos_q = cos_ref[qrows, :]
    sin_q = sin_ref[qrows, :]
    ri = lax.broadcasted_iota(jnp.int32, (QBLK, QBLK), 0)
    ci = lax.broadcasted_iota(jnp.int32, (QBLK, QBLK), 1)
    mask_prev = ci >= ri + jnp.where(i >= 1, 0, QBLK)
    mask_next = ci <= ri - jnp.where(i + 1 < nb, 0, QBLK)
    lane_lo = lax.broadcasted_iota(jnp.int32, (QBLK, LANES), 1) < HEAD_DIM
    jm = jnp.maximum(i - 1, 0)
    jp = jnp.minimum(i + 1, nb - 1)
    for hk in range(KV_HEADS):
        kblk = [kab_s[hk, jm], kab_s[hk, i], kab_s[hk, jp]]
        vblk = [vab_s[hk, jm], vab_s[hk, i], vab_s[hk, jp]]
        ckh = ck_ref[0, hk]
        cvh = cv_ref[0, hk]
        tc = ckh.shape[0] // 2
        for pp in range(2):
            p = 2 * hk + pp
            q2 = (_rope(q_ref[0, :, p * LANES:(p + 1) * LANES], cos_q, sin_q) * ATTN_SCALE).astype(BF16)
            s_loc = [_mm_nt(q2, kb) for kb in kblk]
            s_ctx = _mm_nt(q2, ckh)
            exps, invs = [], []
            for half in range(2):
                cols = slice(half * QBLK, (half + 1) * QBLK)
                parts = [jnp.where(mask_prev, s_loc[0][:, cols], NEG_INF),
                         s_loc[1][:, cols],
                         jnp.where(mask_next, s_loc[2][:, cols], NEG_INF),
                         s_ctx[:, half * tc:(half + 1) * tc]]
                es, inv = _softmax_parts(parts, sink_ref[hk, 2 * pp + half])
                exps.append(es)
                invs.append(inv)
            o = _mm(jnp.concatenate([exps[0][3], exps[1][3]], axis=1), cvh)
            for jj in range(3):
                o = o + _mm(jnp.concatenate([exps[0][jj], exps[1][jj]], axis=1), vblk[jj])
            o_ref[0, :, p * LANES:(p + 1) * LANES] = o * jnp.where(lane_lo, invs[0], invs[1])


def _lat_attention(sink, z3, ck2, cv2, cos_t, sin_t):
    bsz, t, _ = z3.shape
    nb = t // QBLK
    tc2 = ck2.shape[2]
    return pl.pallas_call(
        _lat_attn_kernel,
        grid=(bsz, nb),
        in_specs=[pl.BlockSpec(memory_space=pltpu.SMEM),
                  pl.BlockSpec((1, QBLK, D_MODEL), lambda b, i: (b, i, COL_Q // D_MODEL)),
                  pl.BlockSpec((1, t, KV_WIDTH), lambda b, i: (b, 0, COL_KB // KV_WIDTH)),
                  pl.BlockSpec((1, t, KV_WIDTH), lambda b, i: (b, 0, COL_VB // KV_WIDTH)),
                  pl.BlockSpec((1, KV_HEADS, tc2, LANES), lambda b, i: (b, 0, 0, 0)),
                  pl.BlockSpec((1, KV_HEADS, tc2, LANES), lambda b, i: (b, 0, 0, 0)),
                  pl.BlockSpec((t, LANES), lambda b, i: (0, 0)),
                  pl.BlockSpec((t, LANES), lambda b, i: (0, 0))],
        out_specs=pl.BlockSpec((1, QBLK, D_MODEL), lambda b, i: (b, i, 0)),
        out_shape=jax.ShapeDtypeStruct((bsz, t, D_MODEL), F32),
        scratch_shapes=[pltpu.VMEM((KV_HEADS, nb, 2 * QBLK, LANES), BF16)] * 2,
        compiler_params=_cparams("arbitrary", "arbitrary"),
        name="lat_attn",
    )(sink, z3, z3, z3, ck2, cv2, cos_t, sin_t)


def _merge_kernel(x_ref, mod_ref, ya_ref, yb_ref, ga_ref, gb_ref, ma_ref, mb_ref,
                  woa_ref, wob_ref, wout_ref, fw_ref, o_ref):
    ga = ga_ref[...]
    gb = gb_ref[...]
    br_a = jnp.dot((ya_ref[...] * (ga * _sigmoid(ga))).astype(BF16), woa_ref[...], preferred_element_type=F32)
    br_b = jnp.dot((yb_ref[...] * (gb * _sigmoid(gb))).astype(BF16), wob_ref[...], preferred_element_type=F32)
    merged = _sigmoid(ma_ref[...]) * br_a + _sigmoid(mb_ref[...]) * br_b
    out = x_ref[...] + mod_ref[0, 2:3, :] * jnp.dot(merged.astype(BF16), wout_ref[...],
                                                     preferred_element_type=F32)
    o_ref[...] = out * lax.rsqrt(jnp.mean(out * out, axis=-1, keepdims=True) + RMS_EPS) * fw_ref[...]


def _merge(x2, mod3, ya2, yb2, z2, woa, wob, wout, fw, mod_row, tm):
    m = x2.shape[0]

    def tok(col=0):
        return pl.BlockSpec((tm, D_MODEL), lambda i, col=col: (i, col // D_MODEL))

    def wspec():
        return pl.BlockSpec((D_MODEL, D_MODEL), lambda i: (0, 0))

    return pl.pallas_call(
        _merge_kernel,
        grid=(m // tm,),
        in_specs=[tok(), pl.BlockSpec((1, 3, D_MODEL), lambda i: (mod_row(i), 0, 0)),
                  tok(), tok(), tok(COL_GA), tok(COL_GB), tok(COL_MA), tok(COL_MB),
                  wspec(), wspec(), wspec(), pl.BlockSpec((1, D_MODEL), lambda i: (0, 0))],
        out_specs=tok(),
        out_shape=jax.ShapeDtypeStruct((m, D_MODEL), F32),
        compiler_params=_cparams("arbitrary"),
        name="merge",
    )(x2, mod3, ya2, yb2, z2, z2, z2, z2, woa, wob, wout, fw.reshape(1, D_MODEL))


def _rope_tables(t):
    n_rows = t // GRID_W
    row = jnp.repeat(jnp.arange(n_rows), GRID_W)
    col = jnp.tile(jnp.arange(GRID_W), n_rows)
    nf = HEAD_DIM // 4
    inv = ROPE_BASE ** (-jnp.arange(nf, dtype=F32) / nf)
    ang_r = row.astype(F32)[:, None] * inv[None, :]
    ang_c = col.astype(F32)[:, None] * inv[None, :]
    cos = jnp.concatenate([jnp.cos(ang_r), jnp.cos(ang_r), jnp.cos(ang_c), jnp.cos(ang_c)], axis=-1)
    sin = jnp.concatenate([-jnp.sin(ang_r), jnp.sin(ang_r), -jnp.sin(ang_c), jnp.sin(ang_c)], axis=-1)
    return jnp.tile(cos, (1, 2)), jnp.tile(sin, (1, 2))


def _pad_heads(x):
    zero = jnp.zeros_like(x)
    return jnp.concatenate([jnp.concatenate([x, zero], axis=-1),
                            jnp.concatenate([zero, x], axis=-1)], axis=-2).astype(BF16)


def kernel(x_prompt, x_sample, cache_k, cache_v, state_rwkv, c, c_ctx, norm_w, w_ada, b_ada, w_in, conv_a,
           w0, w_up, a0, a_up, k_k, k_a, r_k, ln_x_w, ln_x_b, w_oA, sink, w_oB, w_out, final_norm_w):
    depth = norm_w.shape[0]
    assert depth == 1
    bc, tc, _ = x_prompt.shape
    bd, td, _ = x_sample.shape
    l = 0
    a_cols = 3 * D_MODEL + 2 * LORA

    wi = w_in[l]
    o_ga = a_cols
    o_q = o_ga + D_MODEL
    o_kb = o_q + D_MODEL
    o_vb = o_kb + KV_WIDTH
    o_gb = o_vb + KV_WIDTH
    o_ma = o_gb + D_MODEL
    o_mb = o_ma + D_MODEL
    w_in_p = jnp.concatenate(
        [wi[:, 0:3 * D_MODEL], wi[:, o_ga:o_q], wi[:, o_q:o_kb], wi[:, o_gb:o_ma], wi[:, o_ma:o_mb],
         wi[:, o_mb:o_mb + D_MODEL], wi[:, o_kb:o_vb], wi[:, o_vb:o_gb], wi[:, 3 * D_MODEL:a_cols]],
        axis=1).astype(BF16)
    conv_rkv = conv_a[l][:, 0:3 * D_MODEL]
    conv_wa = conv_a[l][:, 3 * D_MODEL:a_cols]
    pch = jnp.concatenate(
        [k_k[l][None], k_a[l][None], r_k[l].reshape(1, D_MODEL), ln_x_w[l][None], ln_x_b[l][None],
         0.5 * w0[l], 0.5 * a0[l], jnp.zeros((7, D_MODEL), F32)], axis=0)
    wup = w_up[l].reshape(2, LORA, N_PAIRS, LANES).transpose(0, 2, 1, 3)
    aup = a_up[l].reshape(2, LORA, N_PAIRS, LANES).transpose(0, 2, 1, 3)
    zl = jnp.zeros_like(wup)
    lora = jnp.concatenate([jnp.concatenate([wup, zl], axis=-1),
                            jnp.concatenate([zl, aup], axis=-1)], axis=-2)
    lora = (0.5 * lora).astype(BF16)
    woa, wob, wout = w_oA[l].astype(BF16), w_oB[l].astype(BF16), w_out[l].astype(BF16)
    sink_l = sink[l]

    cond8 = jnp.concatenate([c_ctx[None], c, jnp.zeros((8 - 1 - bd, D_MODEL), F32)], axis=0)
    mod3 = _modulation(cond8, w_ada[l], b_ada[l]).reshape(8, 3, D_MODEL)

    tm = 512
    ctx_row = lambda i: 0
    lat_row = lambda i: 1 + (i * tm) // td

    xp2 = x_prompt.reshape(bc * tc, D_MODEL)
    zc = _inproj(xp2, mod3, norm_w[l], w_in_p, ctx_row, tm)
    zc3 = zc.reshape(bc, tc, ZW)
    ya_c, st_c = _rwkv(zc3, conv_rkv, conv_wa, pch, lora, None, PAIRS_CTX)
    yb_c, new_k, new_v = _ctx_attention(sink_l, zc3)
    y_prompt = _merge(xp2, mod3, ya_c.reshape(bc * tc, D_MODEL), yb_c.reshape(bc * tc, D_MODEL), zc,
                      woa, wob, wout, final_norm_w, ctx_row, tm).reshape(bc, tc, D_MODEL)
    new_state = st_c[:, None]

    xs2 = x_sample.reshape(bd * td, D_MODEL)
    zd = _inproj(xs2, mod3, norm_w[l], w_in_p, lat_row, tm)
    zd3 = zd.reshape(bd, td, ZW)
    s0p = state_rwkv[:, l].reshape(bd, 2, N_PAIRS, 2, HEAD_DIM, HEAD_DIM).transpose(0, 1, 2, 4, 3, 5)
    s0p = s0p.reshape(bd, 2, N_PAIRS, HEAD_DIM, LANES)
    ya_d, _ = _rwkv(zd3, conv_rkv, conv_wa, pch, lora, s0p, PAIRS_LAT)
    cos_t, sin_t = _rope_tables(td)
    yb_d = _lat_attention(sink_l, zd3, _pad_heads(cache_k[:, l]), _pad_heads(cache_v[:, l]), cos_t, sin_t)
    y_sample = _merge(xs2, mod3, ya_d.reshape(bd * td, D_MODEL), yb_d.reshape(bd * td, D_MODEL), zd,
                      woa, wob, wout, final_norm_w, lat_row, tm).reshape(bd, td, D_MODEL)

    return (y_prompt, y_sample, new_k[:, None], new_v[:, None], new_state)
```

```python
import functools
import math

import jax
import jax.numpy as jnp
from jax import lax
from jax.experimental import pallas as pl
from jax.experimental.pallas import tpu as pltpu

F32 = jnp.float32
BF16 = jnp.bfloat16

D_MODEL = 1024
HEAD_DIM = 64
LANES = 128
N_PAIRS = D_MODEL // LANES
KV_HEADS = 4
KV_WIDTH = KV_HEADS * HEAD_DIM
LORA = 64
GRID_W = 64
WINDOW = 128
QBLK = 128
ROPE_BASE = 10000.0
RMS_EPS = 1e-6
GN_EPS = 64e-5
NEG_INF = -1e30
ATTN_SCALE = HEAD_DIM ** -0.5
PAIRS_CTX, PAIRS_LAT = 4, 2
GROUP = 4
HALF_DECAY_SCALE = -0.5 * math.exp(-0.5)
CHUNK = 64

COL_R, COL_K, COL_V, COL_GA, COL_Q, COL_GB, COL_MA, COL_MB = (i * D_MODEL for i in range(8))
COL_KB = 8 * D_MODEL
COL_VB = COL_KB + KV_WIDTH
COL_WA = COL_VB + KV_WIDTH
ZW = COL_WA + 2 * LORA
TN_IN = 2944

VMEM_LIMIT = 56 * 1024 * 1024


def _cparams(*sem):
    return pltpu.CompilerParams(dimension_semantics=sem, vmem_limit_bytes=VMEM_LIMIT)


def _sigmoid(x):
    return 0.5 * jnp.tanh(0.5 * x) + 0.5


def _mm(a, b):
    return jnp.dot(a.astype(BF16), b.astype(BF16), preferred_element_type=F32)


def _mm_nt(a, b):
    return lax.dot_general(a.astype(BF16), b.astype(BF16), (((1,), (1,)), ((), ())),
                           preferred_element_type=F32)


def _mod_kernel(c_ref, w_ref, b_ref, o_ref):
    c = c_ref[...]
    o_ref[...] = jnp.dot(c * _sigmoid(c), w_ref[...], preferred_element_type=F32) + b_ref[...]


def _modulation(cond8, w_ada, b_ada):
    n = w_ada.shape[1]
    tn = D_MODEL
    return pl.pallas_call(
        _mod_kernel,
        grid=(n // tn,),
        in_specs=[pl.BlockSpec((8, D_MODEL), lambda j: (0, 0)),
                  pl.BlockSpec((D_MODEL, tn), lambda j: (0, j)),
                  pl.BlockSpec((1, tn), lambda j: (0, j))],
        out_specs=pl.BlockSpec((8, tn), lambda j: (0, j)),
        out_shape=jax.ShapeDtypeStruct((8, n), F32),
        compiler_params=_cparams("arbitrary"),
        name="mod",
    )(cond8, w_ada, b_ada.reshape(1, n))


def _inproj_kernel(x_ref, mod_ref, nw_ref, w_ref, o_ref):
    x = x_ref[...]
    y = x * lax.rsqrt(jnp.mean(x * x, axis=-1, keepdims=True) + RMS_EPS) * nw_ref[...]
    h = y * (1.0 + mod_ref[0, 1:2, :]) + mod_ref[0, 0:1, :]
    o_ref[...] = jnp.dot(h.astype(BF16), w_ref[...], preferred_element_type=F32)


def _inproj(x2, mod3, norm_w, w_in_p, mod_row, tm):
    m = x2.shape[0]
    return pl.pallas_call(
        _inproj_kernel,
        grid=(ZW // TN_IN, m // tm),
        in_specs=[pl.BlockSpec((tm, D_MODEL), lambda n, i: (i, 0)),
                  pl.BlockSpec((1, 3, D_MODEL), lambda n, i: (mod_row(i), 0, 0)),
                  pl.BlockSpec((1, D_MODEL), lambda n, i: (0, 0)),
                  pl.BlockSpec((D_MODEL, TN_IN), lambda n, i: (0, n))],
        out_specs=pl.BlockSpec((tm, TN_IN), lambda n, i: (i, n)),
        out_shape=jax.ShapeDtypeStruct((m, ZW), F32),
        compiler_params=_cparams("arbitrary", "arbitrary"),
        name="inproj",
    )(x2, mod3, norm_w.reshape(1, D_MODEL), w_in_p)


def _segsum(x, e_bd):
    return jnp.dot(x.astype(BF16), e_bd, preferred_element_type=F32)


def _bmm(a, b):
    return jnp.einsum("bij,bjk->bik", a.astype(BF16), b.astype(BF16), preferred_element_type=F32)


def _bmm_nt(a, b):
    return jnp.einsum("bik,bjk->bij", a.astype(BF16), b.astype(BF16), preferred_element_type=F32)


def _bmm_tn(a, b):
    return jnp.einsum("bki,bkj->bij", a.astype(BF16), b.astype(BF16), preferred_element_type=F32)


def _chunk_local(r, kd, v, a, b, lw):
    c = CHUNK
    n = r.shape[0]
    g = n // 2

    def by_dir(fwd, bwd):
        return jnp.concatenate([fwd, bwd], axis=0)

    def masked(mask_f, mask_r, x):
        return by_dir(jnp.where(mask_f, x[:g], 0.0), jnp.where(mask_r, x[g:], 0.0))

    ti = lax.broadcasted_iota(jnp.int32, (c, c), 0)
    tj = lax.broadcasted_iota(jnp.int32, (c, c), 1)
    tri = by_dir(jnp.broadcast_to(jnp.where(tj <= ti, 1.0, 0.0).astype(BF16), (g, c, c)),
                 jnp.broadcast_to(jnp.where(tj >= ti, 1.0, 0.0).astype(BF16), (g, c, c)))
    lw_hi = lw.astype(BF16)
    lw_lo = (lw - lw_hi.astype(F32)).astype(BF16)
    cs = _bmm(tri, lw_hi) + _bmm(tri, lw_lo)
    tot = by_dir(cs[:g, c - 1:c], cs[g:, 0:1])
    e_in = jnp.exp(cs)
    e_neg = jnp.exp(-cs)
    e_end = jnp.exp(tot - cs)
    at = a * jnp.exp(cs - lw)
    rt = r * e_in
    bt = b * e_neg
    kt = kd * e_neg
    bh = b * e_end
    kh = kd * e_end
    pc = jnp.exp(tot)

    h0 = lax.broadcasted_iota(jnp.int32, (c, LANES), 1) < HEAD_DIM

    def stack2(x):
        m = h0 if x.shape[2] == LANES else jnp.concatenate([h0, h0], axis=1)
        xb = x.astype(BF16)
        zero = jnp.zeros_like(xb)
        return jnp.concatenate([jnp.where(m, xb, zero), jnp.where(m, zero, xb)], axis=1)

    def mul(x, y):
        return _bmm(x, stack2(y))

    gm = _bmm_nt(jnp.concatenate([at, rt], axis=1), jnp.concatenate([stack2(bt), stack2(kt)], axis=1))

    ri = lax.broadcasted_iota(jnp.int32, (c, LANES), 0)
    ci = lax.broadcasted_iota(jnp.int32, (c, LANES), 1) & (c - 1)
    diag = ri == ci

    a_ab = masked(ci < ri, ci > ri, gm[:, 0:c, 0:LANES])
    a_ak = masked(ci < ri, ci > ri, gm[:, 0:c, LANES:])
    a_rb = masked(ci <= ri, ci >= ri, gm[:, c:, 0:LANES])
    a_rk = masked(ci <= ri, ci >= ri, gm[:, c:, LANES:])

    def blk(shift):
        return (ri >> shift) == (ci >> shift)

    eye = jnp.where(diag, 1.0, 0.0)
    a8 = jnp.where(blk(3), a_ab, 0.0)
    a8_2 = mul(a8, a8)
    a8_4 = mul(a8_2, a8_2)
    tm = mul(mul(eye + a8, eye + a8_2), eye + a8_4)
    for sh in (3, 4, 5):
        off = jnp.where(blk(sh + 1) & jnp.logical_not(blk(sh)), a_ab, 0.0)
        tm2 = stack2(tm)
        tm = tm + _bmm(mul(tm, off), tm2)

    v2 = stack2(v)
    wu = _bmm(tm, jnp.concatenate([stack2(at), stack2(mul(a_ak, v))], axis=2))
    wu2 = stack2(wu)
    qy = _bmm(a_rb, wu2)
    qeff = rt + qy[:, :, :LANES]
    yloc = qy[:, :, LANES:] + _bmm(a_rk, v2)
    lhs_t = jnp.concatenate([jnp.concatenate([bh, pltpu.roll(bh, HEAD_DIM, axis=2)], axis=1),
                             jnp.concatenate([kh, pltpu.roll(kh, HEAD_DIM, axis=2)], axis=1)], axis=1)
    rhs = jnp.concatenate([wu2, jnp.concatenate([jnp.zeros_like(v2), v2], axis=2)], axis=1)
    md = _bmm_tn(lhs_t, rhs)[:, :HEAD_DIM]
    mt = md[:, :, :LANES] + jnp.where(diag, pc, 0.0)
    dt = md[:, :, LANES:]
    return qeff, yloc, mt, dt


def _rwkv_kernel(*refs, has_s0, pw):
    if has_s0:
        (r_ref, k_ref, v_ref, wa_ref, cr_ref, ck_ref, cv_ref, cwa_ref, pch_ref, lora_ref, s0_ref,
         ya_ref, so_ref, r_s, v_s, a_s, bon_s, kd_s, b_s, lw_s, y_s, q_s, mt_s, dt_s, z_s) = refs
    else:
        (r_ref, k_ref, v_ref, wa_ref, cr_ref, ck_ref, cv_ref, cwa_ref, pch_ref, lora_ref,
         ya_ref, so_ref, r_s, v_s, a_s, bon_s, kd_s, b_s, lw_s, y_s, q_s, mt_s, dt_s, z_s) = refs
        s0_ref = None
    t = r_ref.shape[1]
    nc = t // CHUNK
    h0 = lax.broadcasted_iota(jnp.int32, (t, LANES), 1) < HEAD_DIM
    ri = lax.broadcasted_iota(jnp.int32, (LANES, LANES), 0)
    ci = lax.broadcasted_iota(jnp.int32, (LANES, LANES), 1)
    e_bd = jnp.where((ri >> 6) == (ci >> 6), 1.0, 0.0).astype(BF16)

    row8 = lax.broadcasted_iota(jnp.int32, (8, LANES), 0)

    def conv(x, cw):
        prev = pltpu.roll(x, 1, axis=0)
        nxt = pltpu.roll(x, t - 1, axis=0)

        def taps(p, c, n):
            return cw[0:1, :] * p + cw[1:2, :] * c + cw[2:3, :] * n

        first = taps(jnp.where(row8 == 0, 0.0, prev[0:8]), x[0:8], nxt[0:8])
        last = taps(prev[t - 8:t], x[t - 8:t], jnp.where(row8 == 7, 0.0, nxt[t - 8:t]))
        mid = taps(prev[8:t - 8], x[8:t - 8], nxt[8:t - 8])
        return jnp.concatenate([first, mid, last], axis=0)

    wa = conv(wa_ref[0], cwa_ref[...])
    lora_in = jnp.where(h0, jnp.tanh(wa), wa).astype(BF16)
    for p in range(pw):
        ls = slice(p * LANES, (p + 1) * LANES)
        pch = pch_ref[:, ls]
        r = conv(r_ref[0, :, ls], cr_ref[:, ls])
        k = conv(k_ref[0, :, ls], ck_ref[:, ls])
        v = conv(v_ref[0, :, ls], cv_ref[:, ls])
        k_k, k_a, r_k = pch[0:1], pch[1:2], pch[2:3]
        kk = k * k_k
        kk = kk * lax.rsqrt(jnp.maximum(_segsum(kk * kk, e_bd), 1e-12))
        r_s[p] = r
        v_s[p] = v
        a_s[p] = -kk
        bon_s[p] = _segsum(r * k * r_k, e_bd) * v
        half_kk = 0.5 * kk
        for d in range(2):
            lo = jnp.dot(lora_in, lora_ref[d, p], preferred_element_type=F32)
            th_w = jnp.tanh(pch[5 + d:6 + d] + lo[:, :LANES])
            th_a = jnp.tanh(pch[7 + d:8 + d] + lo[:, LANES:])
            lw_s[p, d] = HALF_DECAY_SCALE * th_w + HALF_DECAY_SCALE
            kd_s[p, d] = k * ((1.0 - 0.5 * k_a) + (0.5 * k_a) * th_a)
            b_s[p, d] = half_kk * th_a + half_kk

    gsz = GROUP * CHUNK

    def local_body(g, carry):
        rows = pl.ds(pl.multiple_of(g * gsz, gsz), gsz)
        cidx = pl.ds(pl.multiple_of(g * GROUP, GROUP), GROUP)

        def shared(ref):
            x = jnp.concatenate([ref[p, rows, :].reshape(GROUP, CHUNK, LANES) for p in range(pw)], axis=0)
            return jnp.concatenate([x, x], axis=0)

        def per_dir(ref):
            return jnp.concatenate([ref[p, d, rows, :].reshape(GROUP, CHUNK, LANES)
                                    for d in range(2) for p in range(pw)], axis=0)

        qeff, yloc, mt, dt = _chunk_local(shared(r_s), per_dir(kd_s), shared(v_s), shared(a_s),
                                          per_dir(b_s), per_dir(lw_s))
        for d in range(2):
            for p in range(pw):
                part = slice((d * pw + p) * GROUP, (d * pw + p + 1) * GROUP)
                q_s[p, d, rows, :] = qeff[part].reshape(gsz, LANES)
                y_s[p, d, rows, :] = yloc[part].reshape(gsz, LANES)
                mt_s[p, d, cidx] = mt[part]
                dt_s[p, d, cidx] = dt[part]
        return carry

    lax.fori_loop(0, nc // GROUP, local_body, 0)

    hl = lax.broadcasted_iota(jnp.int32, (HEAD_DIM, LANES), 1) < HEAD_DIM

    def stack2(x):
        return jnp.concatenate([jnp.where(hl, x, 0.0), jnp.where(hl, 0.0, x)], axis=0)

    for p in range(pw):
        for d in range(2):
            if s0_ref is None:
                z_s[p, d] = jnp.zeros((HEAD_DIM, LANES), F32)
            else:
                zt = stack2(s0_ref[0, d, p]).T
                z_s[p, d] = zt[:HEAD_DIM] + zt[HEAD_DIM:]

    def seq_body(i, carry):
        for p in range(pw):
            for d in range(2):
                c = i if d == 0 else nc - 1 - i
                sl = pl.ds(pl.multiple_of(c * CHUNK, CHUNK), CHUNK)
                zb = stack2(z_s[p, d]).astype(BF16)
                y_s[p, d, sl, :] = y_s[p, d, sl, :] + jnp.dot(q_s[p, d, sl, :].astype(BF16), zb,
                                                              preferred_element_type=F32)
                z_s[p, d] = jnp.dot(mt_s[p, d, c].astype(BF16), zb, preferred_element_type=F32) + dt_s[p, d, c]
        return carry

    lax.fori_loop(0, nc, seq_body, 0)

    for p in range(pw):
        ls = slice(p * LANES, (p + 1) * LANES)
        for d in range(2):
            zt = stack2(z_s[p, d]).T
            so_ref[0, d, 2 * p] = zt[:HEAD_DIM, :HEAD_DIM]
            so_ref[0, d, 2 * p + 1] = zt[HEAD_DIM:, HEAD_DIM:]
        y = y_s[p, 0] + y_s[p, 1]
        mu = _segsum(y, e_bd) * (1.0 / HEAD_DIM)
        yc = y - mu
        var = _segsum(yc * yc, e_bd) * (1.0 / HEAD_DIM)
        ya_ref[0, :, ls] = (yc * lax.rsqrt(var + GN_EPS) * pch_ref[3:4, ls] + pch_ref[4:5, ls] + bon_s[p])


def _rwkv(z3, conv_rkv, conv_wa, pch, lora, s0p, pw):
    bsz, t, _ = z3.shape
    nc = t // CHUNK
    wl = pw * LANES
    has_s0 = s0p is not None

    def zspec(col):
        return pl.BlockSpec((1, t, wl), lambda b, j, col=col: (b, 0, col // wl + j))

    def cspec(col):
        return pl.BlockSpec((3, wl), lambda b, j, col=col: (0, col // wl + j))

    in_specs = [zspec(COL_R), zspec(COL_K), zspec(COL_V),
                pl.BlockSpec((1, t, LANES), lambda b, j: (b, 0, COL_WA // LANES)),
                cspec(COL_R), cspec(COL_K), cspec(COL_V),
                pl.BlockSpec((3, LANES), lambda b, j: (0, 0)),
                pl.BlockSpec((16, wl), lambda b, j: (0, j)),
                pl.BlockSpec((2, pw, LANES, 2 * LANES), lambda b, j: (0, j, 0, 0))]
    args = [z3, z3, z3, z3, conv_rkv, conv_rkv, conv_rkv, conv_wa, pch, lora]
    if has_s0:
        in_specs.append(pl.BlockSpec((1, 2, pw, HEAD_DIM, LANES), lambda b, j: (b, 0, j, 0, 0)))
        args.append(s0p)
    tl = (t, LANES)
    scratch = [pltpu.VMEM((pw,) + tl, F32)] * 4 + [pltpu.VMEM((pw, 2) + tl, F32)] * 5 \
        + [pltpu.VMEM((pw, 2, nc, HEAD_DIM, LANES), F32)] * 2 + [pltpu.VMEM((pw, 2, HEAD_DIM, LANES), F32)]
    return pl.pallas_call(
        functools.partial(_rwkv_kernel, has_s0=has_s0, pw=pw),
        grid=(bsz, N_PAIRS // pw),
        in_specs=in_specs,
        out_specs=[pl.BlockSpec((1, t, wl), lambda b, j: (b, 0, j)),
                   pl.BlockSpec((1, 2, 2 * pw, HEAD_DIM, HEAD_DIM), lambda b, j: (b, 0, j, 0, 0))],
        out_shape=[jax.ShapeDtypeStruct((bsz, t, D_MODEL), F32),
                   jax.ShapeDtypeStruct((bsz, 2, 2 * N_PAIRS, HEAD_DIM, HEAD_DIM), F32)],
        scratch_shapes=scratch,
        compiler_params=_cparams("arbitrary", "arbitrary"),
        name="rwkv_s0" if has_s0 else "rwkv",
    )(*args)


def _split_heads(x, odd):
    lo = lax.broadcasted_iota(jnp.int32, x.shape, 1) < HEAD_DIM
    if odd:
        xb = jnp.where(lo, 0.0, x)
        return pltpu.roll(xb, HEAD_DIM, axis=1), xb
    xa = jnp.where(lo, x, 0.0)
    return xa, pltpu.roll(xa, HEAD_DIM, axis=1)


def _softmax_parts(parts, sink):
    m = jnp.maximum(functools.reduce(jnp.maximum, [jnp.max(p, axis=-1, keepdims=True) for p in parts]), sink)
    es = [jnp.exp(p - m) for p in parts]
    den = functools.reduce(jnp.add, [jnp.sum(e, axis=-1, keepdims=True) for e in es]) + jnp.exp(sink - m)
    return es, 1.0 / den


def _ctx_attn_kernel(sink_ref, q_ref, k_ref, v_ref, o_ref, nk_ref, nv_ref):
    t = q_ref.shape[1]
    k = k_ref[0]
    v = v_ref[0]
    lane_lo = lax.broadcasted_iota(jnp.int32, (t, LANES), 1) < HEAD_DIM
    for hk in range(KV_HEADS):
        nk_ref[0, hk] = k[:, hk * HEAD_DIM:(hk + 1) * HEAD_DIM]
        nv_ref[0, hk] = v[:, hk * HEAD_DIM:(hk + 1) * HEAD_DIM]
    for hk in range(KV_HEADS):
        cb = hk // 2
        ka, kb = _split_heads(k[:, cb * LANES:(cb + 1) * LANES], hk % 2)
        va, vb = _split_heads(v[:, cb * LANES:(cb + 1) * LANES], hk % 2)
        kab = jnp.concatenate([ka, kb], axis=0).astype(BF16)
        vab = jnp.concatenate([va, vb], axis=0).astype(BF16)
        for pp in range(2):
            p = 2 * hk + pp
            q2 = (q_ref[0, :, p * LANES:(p + 1) * LANES] * ATTN_SCALE).astype(BF16)
            s = _mm_nt(q2, kab)
            e0, i0 = _softmax_parts([s[:, :t]], sink_ref[hk, 2 * pp])
            e1, i1 = _softmax_parts([s[:, t:]], sink_ref[hk, 2 * pp + 1])
            o = _mm(jnp.concatenate([e0[0], e1[0]], axis=1), vab)
            o_ref[0, :, p * LANES:(p + 1) * LANES] = o * jnp.where(lane_lo, i0, i1)


def _ctx_attention(sink, z3):
    bsz, t, _ = z3.shape
    return pl.pallas_call(
        _ctx_attn_kernel,
        grid=(bsz,),
        in_specs=[pl.BlockSpec(memory_space=pltpu.SMEM),
                  pl.BlockSpec((1, t, D_MODEL), lambda b: (b, 0, COL_Q // D_MODEL)),
                  pl.BlockSpec((1, t, KV_WIDTH), lambda b: (b, 0, COL_KB // KV_WIDTH)),
                  pl.BlockSpec((1, t, KV_WIDTH), lambda b: (b, 0, COL_VB // KV_WIDTH))],
        out_specs=[pl.BlockSpec((1, t, D_MODEL), lambda b: (b, 0, 0)),
                   pl.BlockSpec((1, KV_HEADS, t, HEAD_DIM), lambda b: (b, 0, 0, 0)),
                   pl.BlockSpec((1, KV_HEADS, t, HEAD_DIM), lambda b: (b, 0, 0, 0))],
        out_shape=[jax.ShapeDtypeStruct((bsz, t, D_MODEL), F32),
                   jax.ShapeDtypeStruct((bsz, KV_HEADS, t, HEAD_DIM), F32),
                   jax.ShapeDtypeStruct((bsz, KV_HEADS, t, HEAD_DIM), F32)],
        compiler_params=_cparams("arbitrary"),
        name="ctx_attn",
    )(sink, z3, z3, z3)


def _rope(x, cos, sin_signed):
    lane = lax.broadcasted_iota(jnp.int32, x.shape, 1)
    partner = jnp.where((lane & 31) < 16, pltpu.roll(x, LANES - 16, axis=1), pltpu.roll(x, 16, axis=1))
    return x * cos + partner * sin_signed


def _lat_attn_kernel(sink_ref, q_ref, k_ref, v_ref, ck_ref, cv_ref, cos_ref, sin_ref, o_ref, kab_s, vab_s):
    t = k_ref.shape[1]
    nb = t // QBLK
    i = pl.program_id(1)

    @pl.when(i == 0)
    def _():
        cos = cos_ref[...]
        sin = sin_ref[...]
        for cb in range(KV_HEADS // 2):
            kr = _rope(k_ref[0, :, cb * LANES:(cb + 1) * LANES], cos, sin)
            vv = v_ref[0, :, cb * LANES:(cb + 1) * LANES]
            for odd in range(2):
                hk = 2 * cb + odd
                ka, kb = _split_heads(kr, odd)
                va, vb = _split_heads(vv, odd)
                for j in range(nb):
                    rows = slice(j * QBLK, (j + 1) * QBLK)
                    kab_s[hk, j, 0:QBLK, :] = ka[rows].astype(BF16)
                    kab_s[hk, j, QBLK:2 * QBLK, :] = kb[rows].astype(BF16)
                    vab_s[hk, j, 0:QBLK, :] = va[rows].astype(BF16)
                    vab_s[hk, j, QBLK:2 * QBLK, :] = vb[rows].astype(BF16)

    qrows = pl.ds(pl.multiple_of(i * QBLK, QBLK), QBLK)
    cos_q = cos_ref[qrows, :]
    sin_q = sin_ref[qrows, :]
    ri = lax.broadcasted_iota(jnp.int32, (QBLK, QBLK), 0)
    ci = lax.broadcasted_iota(jnp.int32, (QBLK, QBLK), 1)
    mask_prev = ci >= ri + jnp.where(i >= 1, 0, QBLK)
    mask_next = ci <= ri - jnp.where(i + 1 < nb, 0, QBLK)
    lane_lo = lax.broadcasted_iota(jnp.int32, (QBLK, LANES), 1) < HEAD_DIM
    jm = jnp.maximum(i - 1, 0)
    jp = jnp.minimum(i + 1, nb - 1)
    for hk in range(KV_HEADS):
        kblk = [kab_s[hk, jm], kab_s[hk, i], kab_s[hk, jp]]
        vblk = [vab_s[hk, jm], vab_s[hk, i], vab_s[hk, jp]]
        ckh = ck_ref[0, hk]
        cvh = cv_ref[0, hk]
        tc = ckh.shape[0] // 2
        for pp in range(2):
            p = 2 * hk + pp
            q2 = (_rope(q_ref[0, :, p * LANES:(p + 1) * LANES], cos_q, sin_q) * ATTN_SCALE).astype(BF16)
            s_loc = [_mm_nt(q2, kb) for kb in kblk]
            s_ctx = _mm_nt(q2, ckh)
            exps, invs = [], []
            for half in range(2):
                cols = slice(half * QBLK, (half + 1) * QBLK)
                parts = [jnp.where(mask_prev, s_loc[0][:, cols], NEG_INF),
                         s_loc[1][:, cols],
                         jnp.where(mask_next, s_loc[2][:, cols], NEG_INF),
                         s_ctx[:, half * tc:(half + 1) * tc]]
                es, inv = _softmax_parts(parts, sink_ref[hk, 2 * pp + half])
                exps.append(es)
                invs.append(inv)
            o = _mm(jnp.concatenate([exps[0][3], exps[1][3]], axis=1), cvh)
            for jj in range(3):
                o = o + _mm(jnp.concatenate([exps[0][jj], exps[1][jj]], axis=1), vblk[jj])
            o_ref[0, :, p * LANES:(p + 1) * LANES] = o * jnp.where(lane_lo, invs[0], invs[1])


def _lat_attention(sink, z3, ck2, cv2, cos_t, sin_t):
    bsz, t, _ = z3.shape
    nb = t // QBLK
    tc2 = ck2.shape[2]
    return pl.pallas_call(
        _lat_attn_kernel,
        grid=(bsz, nb),
        in_specs=[pl.BlockSpec(memory_space=pltpu.SMEM),
                  pl.BlockSpec((1, QBLK, D_MODEL), lambda b, i: (b, i, COL_Q // D_MODEL)),
                  pl.BlockSpec((1, t, KV_WIDTH), lambda b, i: (b, 0, COL_KB // KV_WIDTH)),
                  pl.BlockSpec((1, t, KV_WIDTH), lambda b, i: (b, 0, COL_VB // KV_WIDTH)),
                  pl.BlockSpec((1, KV_HEADS, tc2, LANES), lambda b, i: (b, 0, 0, 0)),
                  pl.BlockSpec((1, KV_HEADS, tc2, LANES), lambda b, i: (b, 0, 0, 0)),
                  pl.BlockSpec((t, LANES), lambda b, i: (0, 0)),
                  pl.BlockSpec((t, LANES), lambda b, i: (0, 0))],
        out_specs=pl.BlockSpec((1, QBLK, D_MODEL), lambda b, i: (b, i, 0)),
        out_shape=jax.ShapeDtypeStruct((bsz, t, D_MODEL), F32),
        scratch_shapes=[pltpu.VMEM((KV_HEADS, nb, 2 * QBLK, LANES), BF16)] * 2,
        compiler_params=_cparams("arbitrary", "arbitrary"),
        name="lat_attn",
    )(sink, z3, z3, z3, ck2, cv2, cos_t, sin_t)


def _merge_kernel(x_ref, mod_ref, ya_ref, yb_ref, ga_ref, gb_ref, ma_ref, mb_ref,
                  woa_ref, wob_ref, wout_ref, fw_ref, o_ref):
    ga = ga_ref[...]
    gb = gb_ref[...]
    br_a = jnp.dot((ya_ref[...] * (ga * _sigmoid(ga))).astype(BF16), woa_ref[...], preferred_element_type=F32)
    br_b = jnp.dot((yb_ref[...] * (gb * _sigmoid(gb))).astype(BF16), wob_ref[...], preferred_element_type=F32)
    merged = _sigmoid(ma_ref[...]) * br_a + _sigmoid(mb_ref[...]) * br_b
    out = x_ref[...] + mod_ref[0, 2:3, :] * jnp.dot(merged.astype(BF16), wout_ref[...],
                                                     preferred_element_type=F32)
    o_ref[...] = out * lax.rsqrt(jnp.mean(out * out, axis=-1, keepdims=True) + RMS_EPS) * fw_ref[...]


def _merge(x2, mod3, ya2, yb2, z2, woa, wob, wout, fw, mod_row, tm):
    m = x2.shape[0]

    def tok(col=0):
        return pl.BlockSpec((tm, D_MODEL), lambda i, col=col: (i, col // D_MODEL))

    def wspec():
        return pl.BlockSpec((D_MODEL, D_MODEL), lambda i: (0, 0))

    return pl.pallas_call(
        _merge_kernel,
        grid=(m // tm,),
        in_specs=[tok(), pl.BlockSpec((1, 3, D_MODEL), lambda i: (mod_row(i), 0, 0)),
                  tok(), tok(), tok(COL_GA), tok(COL_GB), tok(COL_MA), tok(COL_MB),
                  wspec(), wspec(), wspec(), pl.BlockSpec((1, D_MODEL), lambda i: (0, 0))],
        out_specs=tok(),
        out_shape=jax.ShapeDtypeStruct((m, D_MODEL), F32),
        compiler_params=_cparams("arbitrary"),
        name="merge",
    )(x2, mod3, ya2, yb2, z2, z2, z2, z2, woa, wob, wout, fw.reshape(1, D_MODEL))


def _rope_tables(t):
    n_rows = t // GRID_W
    row = jnp.repeat(jnp.arange(n_rows), GRID_W)
    col = jnp.tile(jnp.arange(GRID_W), n_rows)
    nf = HEAD_DIM // 4
    inv = ROPE_BASE ** (-jnp.arange(nf, dtype=F32) / nf)
    ang_r = row.astype(F32)[:, None] * inv[None, :]
    ang_c = col.astype(F32)[:, None] * inv[None, :]
    cos = jnp.concatenate([jnp.cos(ang_r), jnp.cos(ang_r), jnp.cos(ang_c), jnp.cos(ang_c)], axis=-1)
    sin = jnp.concatenate([-jnp.sin(ang_r), jnp.sin(ang_r), -jnp.sin(ang_c), jnp.sin(ang_c)], axis=-1)
    return jnp.tile(cos, (1, 2)), jnp.tile(sin, (1, 2))


def _pad_heads(x):
    zero = jnp.zeros_like(x)
    return jnp.concatenate([jnp.concatenate([x, zero], axis=-1),
                            jnp.concatenate([zero, x], axis=-1)], axis=-2).astype(BF16)


def kernel(x_prompt, x_sample, cache_k, cache_v, state_rwkv, c, c_ctx, norm_w, w_ada, b_ada, w_in, conv_a,
           w0, w_up, a0, a_up, k_k, k_a, r_k, ln_x_w, ln_x_b, w_oA, sink, w_oB, w_out, final_norm_w):
    depth = norm_w.shape[0]
    assert depth == 1
    bc, tc, _ = x_prompt.shape
    bd, td, _ = x_sample.shape
    l = 0
    a_cols = 3 * D_MODEL + 2 * LORA

    wi = w_in[l]
    o_ga = a_cols
    o_q = o_ga + D_MODEL
    o_kb = o_q + D_MODEL
    o_vb = o_kb + KV_WIDTH
    o_gb = o_vb + KV_WIDTH
    o_ma = o_gb + D_MODEL
    o_mb = o_ma + D_MODEL
    w_in_p = jnp.concatenate(
        [wi[:, 0:3 * D_MODEL], wi[:, o_ga:o_q], wi[:, o_q:o_kb], wi[:, o_gb:o_ma], wi[:, o_ma:o_mb],
         wi[:, o_mb:o_mb + D_MODEL], wi[:, o_kb:o_vb], wi[:, o_vb:o_gb], wi[:, 3 * D_MODEL:a_cols]],
        axis=1).astype(BF16)
    conv_rkv = conv_a[l][:, 0:3 * D_MODEL]
    conv_wa = conv_a[l][:, 3 * D_MODEL:a_cols]
    pch = jnp.concatenate(
        [k_k[l][None], k_a[l][None], r_k[l].reshape(1, D_MODEL), ln_x_w[l][None], ln_x_b[l][None],
         0.5 * w0[l], 0.5 * a0[l], jnp.zeros((7, D_MODEL), F32)], axis=0)
    wup = w_up[l].reshape(2, LORA, N_PAIRS, LANES).transpose(0, 2, 1, 3)
    aup = a_up[l].reshape(2, LORA, N_PAIRS, LANES).transpose(0, 2, 1, 3)
    zl = jnp.zeros_like(wup)
    lora = jnp.concatenate([jnp.concatenate([wup, zl], axis=-1),
                            jnp.concatenate([zl, aup], axis=-1)], axis=-2)
    lora = (0.5 * lora).astype(BF16)
    woa, wob, wout = w_oA[l].astype(BF16), w_oB[l].astype(BF16), w_out[l].astype(BF16)
    sink_l = sink[l]

    cond8 = jnp.concatenate([c_ctx[None], c, jnp.zeros((8 - 1 - bd, D_MODEL), F32)], axis=0)
    mod3 = _modulation(cond8, w_ada[l], b_ada[l]).reshape(8, 3, D_MODEL)

    tm = 512
    ctx_row = lambda i: 0
    lat_row = lambda i: 1 + (i * tm) // td

    xp2 = x_prompt.reshape(bc * tc, D_MODEL)
    zc = _inproj(xp2, mod3, norm_w[l], w_in_p, ctx_row, tm)
    zc3 = zc.reshape(bc, tc, ZW)
    ya_c, st_c = _rwkv(zc3, conv_rkv, conv_wa, pch, lora, None, PAIRS_CTX)
    yb_c, new_k, new_v = _ctx_attention(sink_l, zc3)
    y_prompt = _merge(xp2, mod3, ya_c.reshape(bc * tc, D_MODEL), yb_c.reshape(bc * tc, D_MODEL), zc,
                      woa, wob, wout, final_norm_w, ctx_row, tm).reshape(bc, tc, D_MODEL)
    new_state = st_c[:, None]

    xs2 = x_sample.reshape(bd * td, D_MODEL)
    zd = _inproj(xs2, mod3, norm_w[l], w_in_p, lat_row, tm)
    zd3 = zd.reshape(bd, td, ZW)
    s0p = state_rwkv[:, l].reshape(bd, 2, N_PAIRS, 2, HEAD_DIM, HEAD_DIM).transpose(0, 1, 2, 4, 3, 5)
    s0p = s0p.reshape(bd, 2, N_PAIRS, HEAD_DIM, LANES)
    ya_d, _ = _rwkv(zd3, conv_rkv, conv_wa, pch, lora, s0p, PAIRS_LAT)
    cos_t, sin_t = _rope_tables(td)
    yb_d = _lat_attention(sink_l, zd3, _pad_heads(cache_k[:, l]), _pad_heads(cache_v[:, l]), cos_t, sin_t)
    y_sample = _merge(xs2, mod3, ya_d.reshape(bd * td, D_MODEL), yb_d.reshape(bd * td, D_MODEL), zd,
                      woa, wob, wout, final_norm_w, lat_row, tm).reshape(bd, td, D_MODEL)

    return (y_prompt, y_sample, new_k[:, None], new_v[:, None], new_state)
```

```python
import functools
import math

import jax
import jax.numpy as jnp
from jax import lax
from jax.experimental import pallas as pl
from jax.experimental.pallas import tpu as pltpu

F32 = jnp.float32
BF16 = jnp.bfloat16

D_MODEL = 1024
HEAD_DIM = 64
LANES = 128
N_PAIRS = D_MODEL // LANES
KV_HEADS = 4
KV_WIDTH = KV_HEADS * HEAD_DIM
LORA = 64
GRID_W = 64
WINDOW = 128
QBLK = 128
ROPE_BASE = 10000.0
RMS_EPS = 1e-6
GN_EPS = 64e-5
NEG_INF = -1e30
ATTN_SCALE = HEAD_DIM ** -0.5
PAIRS_CTX, PAIRS_LAT = 4, 2
GROUP = 4
HALF_DECAY_SCALE = -0.5 * math.exp(-0.5)
CHUNK = 64

COL_R, COL_K, COL_V, COL_GA, COL_Q, COL_GB, COL_MA, COL_MB = (i * D_MODEL for i in range(8))
COL_KB = 8 * D_MODEL
COL_VB = COL_KB + KV_WIDTH
COL_WA = COL_VB + KV_WIDTH
ZW = COL_WA + 2 * LORA
TN_IN = 2944

VMEM_LIMIT = 56 * 1024 * 1024


def _cparams(*sem):
    return pltpu.CompilerParams(dimension_semantics=sem, vmem_limit_bytes=VMEM_LIMIT)


def _sigmoid(x):
    return 0.5 * jnp.tanh(0.5 * x) + 0.5


def _mm(a, b):
    return jnp.dot(a.astype(BF16), b.astype(BF16), preferred_element_type=F32)


def _mm_nt(a, b):
    return lax.dot_general(a.astype(BF16), b.astype(BF16), (((1,), (1,)), ((), ())),
                           preferred_element_type=F32)


def _mod_kernel(c_ref, w_ref, b_ref, o_ref):
    c = c_ref[...]
    o_ref[...] = jnp.dot(c * _sigmoid(c), w_ref[...], preferred_element_type=F32) + b_ref[...]


def _modulation(cond8, w_ada, b_ada):
    n = w_ada.shape[1]
    tn = D_MODEL
    return pl.pallas_call(
        _mod_kernel,
        grid=(n // tn,),
        in_specs=[pl.BlockSpec((8, D_MODEL), lambda j: (0, 0)),
                  pl.BlockSpec((D_MODEL, tn), lambda j: (0, j)),
                  pl.BlockSpec((1, tn), lambda j: (0, j))],
        out_specs=pl.BlockSpec((8, tn), lambda j: (0, j)),
        out_shape=jax.ShapeDtypeStruct((8, n), F32),
        compiler_params=_cparams("arbitrary"),
        name="mod",
    )(cond8, w_ada, b_ada.reshape(1, n))


def _inproj_kernel(x_ref, mod_ref, nw_ref, w_ref, o_ref):
    x = x_ref[...]
    y = x * lax.rsqrt(jnp.mean(x * x, axis=-1, keepdims=True) + RMS_EPS) * nw_ref[...]
    h = y * (1.0 + mod_ref[0, 1:2, :]) + mod_ref[0, 0:1, :]
    o_ref[...] = jnp.dot(h.astype(BF16), w_ref[...], preferred_element_type=F32)


def _inproj(x2, mod3, norm_w, w_in_p, mod_row, tm):
    m = x2.shape[0]
    return pl.pallas_call(
        _inproj_kernel,
        grid=(ZW // TN_IN, m // tm),
        in_specs=[pl.BlockSpec((tm, D_MODEL), lambda n, i: (i, 0)),
                  pl.BlockSpec((1, 3, D_MODEL), lambda n, i: (mod_row(i), 0, 0)),
                  pl.BlockSpec((1, D_MODEL), lambda n, i: (0, 0)),
                  pl.BlockSpec((D_MODEL, TN_IN), lambda n, i: (0, n))],
        out_specs=pl.BlockSpec((tm, TN_IN), lambda n, i: (i, n)),
        out_shape=jax.ShapeDtypeStruct((m, ZW), F32),
        compiler_params=_cparams("arbitrary", "arbitrary"),
        name="inproj",
    )(x2, mod3, norm_w.reshape(1, D_MODEL), w_in_p)


def _segsum(x, e_bd):
    return jnp.dot(x.astype(BF16), e_bd, preferred_element_type=F32)


def _bmm(a, b):
    return jnp.einsum("bij,bjk->bik", a.astype(BF16), b.astype(BF16), preferred_element_type=F32)


def _bmm_nt(a, b):
    return jnp.einsum("bik,bjk->bij", a.astype(BF16), b.astype(BF16), preferred_element_type=F32)


def _bmm_tn(a, b):
    return jnp.einsum("bki,bkj->bij", a.astype(BF16), b.astype(BF16), preferred_element_type=F32)


def _chunk_local(r, kd, v, a, b, lw):
    c = CHUNK
    n = r.shape[0]
    g = n // 2

    def by_dir(fwd, bwd):
        return jnp.concatenate([fwd, bwd], axis=0)

    def masked(mask_f, mask_r, x):
        return by_dir(jnp.where(mask_f, x[:g], 0.0), jnp.where(mask_r, x[g:], 0.0))

    ti = lax.broadcasted_iota(jnp.int32, (c, c), 0)
    tj = lax.broadcasted_iota(jnp.int32, (c, c), 1)
    tri = by_dir(jnp.broadcast_to(jnp.where(tj <= ti, 1.0, 0.0).astype(BF16), (g, c, c)),
                 jnp.broadcast_to(jnp.where(tj >= ti, 1.0, 0.0).astype(BF16), (g, c, c)))
    lw_hi = lw.astype(BF16)
    lw_lo = (lw - lw_hi.astype(F32)).astype(BF16)
    cs = _bmm(tri, lw_hi) + _bmm(tri, lw_lo)
    tot = by_dir(cs[:g, c - 1:c], cs[g:, 0:1])
    e_in = jnp.exp(cs)
    e_neg = jnp.exp(-cs)
    pc = jnp.exp(tot)
    e_end = pc * e_neg
    at = a * jnp.exp(cs - lw)
    rt = r * e_in
    bt = b * e_neg
    kt = kd * e_neg
    bh = b * e_end
    kh = kd * e_end

    h0 = lax.broadcasted_iota(jnp.int32, (c, LANES), 1) < HEAD_DIM

    def stack2(x):
        m = h0 if x.shape[2] == LANES else jnp.concatenate([h0, h0], axis=1)
        xb = x.astype(BF16)
        zero = jnp.zeros_like(xb)
        return jnp.concatenate([jnp.where(m, xb, zero), jnp.where(m, zero, xb)], axis=1)

    def mul(x, y):
        return _bmm(x, stack2(y))

    gm = _bmm_nt(jnp.concatenate([at, rt], axis=1), jnp.concatenate([stack2(bt), stack2(kt)], axis=1))

    ri = lax.broadcasted_iota(jnp.int32, (c, LANES), 0)
    ci = lax.broadcasted_iota(jnp.int32, (c, LANES), 1) & (c - 1)
    diag = ri == ci

    a_ab = masked(ci < ri, ci > ri, gm[:, 0:c, 0:LANES])
    a_ak = masked(ci < ri, ci > ri, gm[:, 0:c, LANES:])
    a_rb = masked(ci <= ri, ci >= ri, gm[:, c:, 0:LANES])
    a_rk = masked(ci <= ri, ci >= ri, gm[:, c:, LANES:])

    def blk(shift):
        return (ri >> shift) == (ci >> shift)

    eye = jnp.where(diag, 1.0, 0.0)
    a8 = jnp.where(blk(3), a_ab, 0.0)
    a8_2 = mul(a8, a8)
    a8_4 = mul(a8_2, a8_2)
    tm = mul(mul(eye + a8, eye + a8_2), eye + a8_4)
    for sh in (3, 4, 5):
        off = jnp.where(blk(sh + 1) & jnp.logical_not(blk(sh)), a_ab, 0.0)
        tm2 = stack2(tm)
        tm = tm + _bmm(mul(tm, off), tm2)

    v2 = stack2(v)
    wu = _bmm(tm, jnp.concatenate([stack2(at), stack2(mul(a_ak, v))], axis=2))
    wu2 = stack2(wu)
    qy = _bmm(a_rb, wu2)
    qeff = rt + qy[:, :, :LANES]
    yloc = qy[:, :, LANES:] + _bmm(a_rk, v2)
    lhs_t = jnp.concatenate([jnp.concatenate([bh, pltpu.roll(bh, HEAD_DIM, axis=2)], axis=1),
                             jnp.concatenate([kh, pltpu.roll(kh, HEAD_DIM, axis=2)], axis=1)], axis=1)
    rhs = jnp.concatenate([wu2, jnp.concatenate([jnp.zeros_like(v2), v2], axis=2)], axis=1)
    md = _bmm_tn(lhs_t, rhs)[:, :HEAD_DIM]
    mt = md[:, :, :LANES] + jnp.where(diag, pc, 0.0)
    dt = md[:, :, LANES:]
    return qeff, yloc, mt, dt


def _rwkv_kernel(*refs, has_s0, pw):
    if has_s0:
        (r_ref, k_ref, v_ref, wa_ref, cr_ref, ck_ref, cv_ref, cwa_ref, pch_ref, lora_ref, s0_ref,
         ya_ref, so_ref, r_s, v_s, a_s, bon_s, kd_s, b_s, lw_s, y_s, q_s, mt_s, dt_s, z_s) = refs
    else:
        (r_ref, k_ref, v_ref, wa_ref, cr_ref, ck_ref, cv_ref, cwa_ref, pch_ref, lora_ref,
         ya_ref, so_ref, r_s, v_s, a_s, bon_s, kd_s, b_s, lw_s, y_s, q_s, mt_s, dt_s, z_s) = refs
        s0_ref = None
    t = r_ref.shape[1]
    nc = t // CHUNK
    h0 = lax.broadcasted_iota(jnp.int32, (t, LANES), 1) < HEAD_DIM
    ri = lax.broadcasted_iota(jnp.int32, (LANES, LANES), 0)
    ci = lax.broadcasted_iota(jnp.int32, (LANES, LANES), 1)
    e_bd = jnp.where((ri >> 6) == (ci >> 6), 1.0, 0.0).astype(BF16)

    row8 = lax.broadcasted_iota(jnp.int32, (8, LANES), 0)

    def conv(x, cw):
        prev = pltpu.roll(x, 1, axis=0)
        nxt = pltpu.roll(x, t - 1, axis=0)

        def taps(p, c, n):
            return cw[0:1, :] * p + cw[1:2, :] * c + cw[2:3, :] * n

        first = taps(jnp.where(row8 == 0, 0.0, prev[0:8]), x[0:8], nxt[0:8])
        last = taps(prev[t - 8:t], x[t - 8:t], jnp.where(row8 == 7, 0.0, nxt[t - 8:t]))
        mid = taps(prev[8:t - 8], x[8:t - 8], nxt[8:t - 8])
        return jnp.concatenate([first, mid, last], axis=0)

    wa = conv(wa_ref[0], cwa_ref[...])
    lora_in = jnp.where(h0, jnp.tanh(wa), wa).astype(BF16)
    for p in range(pw):
        ls = slice(p * LANES, (p + 1) * LANES)
        pch = pch_ref[:, ls]
        r = conv(r_ref[0, :, ls], cr_ref[:, ls])
        k = conv(k_ref[0, :, ls], ck_ref[:, ls])
        v = conv(v_ref[0, :, ls], cv_ref[:, ls])
        k_k, k_a, r_k = pch[0:1], pch[1:2], pch[2:3]
        kk = k * k_k
        kk = kk * lax.rsqrt(jnp.maximum(_segsum(kk * kk, e_bd), 1e-12))
        r_s[p] = r
        v_s[p] = v
        a_s[p] = -kk
        bon_s[p] = _segsum(r * k * r_k, e_bd) * v
        half_kk = 0.5 * kk
        for d in range(2):
            lo = jnp.dot(lora_in, lora_ref[d, p], preferred_element_type=F32)
            th_w = jnp.tanh(pch[5 + d:6 + d] + lo[:, :LANES])
            th_a = jnp.tanh(pch[7 + d:8 + d] + lo[:, LANES:])
            lw_s[p, d] = HALF_DECAY_SCALE * th_w + HALF_DECAY_SCALE
            kd_s[p, d] = k * ((1.0 - 0.5 * k_a) + (0.5 * k_a) * th_a)
            b_s[p, d] = half_kk * th_a + half_kk

    gsz = GROUP * CHUNK

    def local_body(g, carry):
        rows = pl.ds(pl.multiple_of(g * gsz, gsz), gsz)
        cidx = pl.ds(pl.multiple_of(g * GROUP, GROUP), GROUP)

        def shared(ref):
            x = jnp.concatenate([ref[p, rows, :].reshape(GROUP, CHUNK, LANES) for p in range(pw)], axis=0)
            return jnp.concatenate([x, x], axis=0)

        def per_dir(ref):
            return jnp.concatenate([ref[p, d, rows, :].reshape(GROUP, CHUNK, LANES)
                                    for d in range(2) for p in range(pw)], axis=0)

        qeff, yloc, mt, dt = _chunk_local(shared(r_s), per_dir(kd_s), shared(v_s), shared(a_s),
                                          per_dir(b_s), per_dir(lw_s))
        for d in range(2):
            for p in range(pw):
                part = slice((d * pw + p) * GROUP, (d * pw + p + 1) * GROUP)
                q_s[p, d, rows, :] = qeff[part].reshape(gsz, LANES)
                y_s[p, d, rows, :] = yloc[part].reshape(gsz, LANES)
                mt_s[p, d, cidx] = mt[part]
                dt_s[p, d, cidx] = dt[part]
        return carry

    lax.fori_loop(0, nc // GROUP, local_body, 0)

    hl = lax.broadcasted_iota(jnp.int32, (HEAD_DIM, LANES), 1) < HEAD_DIM

    def stack2(x):
        return jnp.concatenate([jnp.where(hl, x, 0.0), jnp.where(hl, 0.0, x)], axis=0)

    for p in range(pw):
        for d in range(2):
            if s0_ref is None:
                z_s[p, d] = jnp.zeros((HEAD_DIM, LANES), F32)
            else:
                zt = stack2(s0_ref[0, d, p]).T
                z_s[p, d] = zt[:HEAD_DIM] + zt[HEAD_DIM:]

    def seq_body(i, carry):
        for p in range(pw):
            for d in range(2):
                c = i if d == 0 else nc - 1 - i
                sl = pl.ds(pl.multiple_of(c * CHUNK, CHUNK), CHUNK)
                zb = stack2(z_s[p, d]).astype(BF16)
                y_s[p, d, sl, :] = y_s[p, d, sl, :] + jnp.dot(q_s[p, d, sl, :].astype(BF16), zb,
                                                              preferred_element_type=F32)
                z_s[p, d] = jnp.dot(mt_s[p, d, c].astype(BF16), zb, preferred_element_type=F32) + dt_s[p, d, c]
        return carry

    lax.fori_loop(0, nc, seq_body, 0)

    for p in range(pw):
        ls = slice(p * LANES, (p + 1) * LANES)
        for d in range(2):
            zt = stack2(z_s[p, d]).T
            so_ref[0, d, 2 * p] = zt[:HEAD_DIM, :HEAD_DIM]
            so_ref[0, d, 2 * p + 1] = zt[HEAD_DIM:, HEAD_DIM:]
        y = y_s[p, 0] + y_s[p, 1]
        mu = _segsum(y, e_bd) * (1.0 / HEAD_DIM)
        yc = y - mu
        var = _segsum(yc * yc, e_bd) * (1.0 / HEAD_DIM)
        ya_ref[0, :, ls] = (yc * lax.rsqrt(var + GN_EPS) * pch_ref[3:4, ls] + pch_ref[4:5, ls] + bon_s[p])


def _rwkv(z3, conv_rkv, conv_wa, pch, lora, s0p, pw):
    bsz, t, _ = z3.shape
    nc = t // CHUNK
    wl = pw * LANES
    has_s0 = s0p is not None

    def zspec(col):
        return pl.BlockSpec((1, t, wl), lambda b, j, col=col: (b, 0, col // wl + j))

    def cspec(col):
        return pl.BlockSpec((3, wl), lambda b, j, col=col: (0, col // wl + j))

    in_specs = [zspec(COL_R), zspec(COL_K), zspec(COL_V),
                pl.BlockSpec((1, t, LANES), lambda b, j: (b, 0, COL_WA // LANES)),
                cspec(COL_R), cspec(COL_K), cspec(COL_V),
                pl.BlockSpec((3, LANES), lambda b, j: (0, 0)),
                pl.BlockSpec((16, wl), lambda b, j: (0, j)),
                pl.BlockSpec((2, pw, LANES, 2 * LANES), lambda b, j: (0, j, 0, 0))]
    args = [z3, z3, z3, z3, conv_rkv, conv_rkv, conv_rkv, conv_wa, pch, lora]
    if has_s0:
        in_specs.append(pl.BlockSpec((1, 2, pw, HEAD_DIM, LANES), lambda b, j: (b, 0, j, 0, 0)))
        args.append(s0p)
    tl = (t, LANES)
    scratch = [pltpu.VMEM((pw,) + tl, F32)] * 4 + [pltpu.VMEM((pw, 2) + tl, F32)] * 5 \
        + [pltpu.VMEM((pw, 2, nc, HEAD_DIM, LANES), F32)] * 2 + [pltpu.VMEM((pw, 2, HEAD_DIM, LANES), F32)]
    return pl.pallas_call(
        functools.partial(_rwkv_kernel, has_s0=has_s0, pw=pw),
        grid=(bsz, N_PAIRS // pw),
        in_specs=in_specs,
        out_specs=[pl.BlockSpec((1, t, wl), lambda b, j: (b, 0, j)),
                   pl.BlockSpec((1, 2, 2 * pw, HEAD_DIM, HEAD_DIM), lambda b, j: (b, 0, j, 0, 0))],
        out_shape=[jax.ShapeDtypeStruct((bsz, t, D_MODEL), F32),
                   jax.ShapeDtypeStruct((bsz, 2, 2 * N_PAIRS, HEAD_DIM, HEAD_DIM), F32)],
        scratch_shapes=scratch,
        compiler_params=_cparams("arbitrary", "arbitrary"),
        name="rwkv_s0" if has_s0 else "rwkv",
    )(*args)


def _split_heads(x, odd):
    lo = lax.broadcasted_iota(jnp.int32, x.shape, 1) < HEAD_DIM
    if odd:
        xb = jnp.where(lo, 0.0, x)
        return pltpu.roll(xb, HEAD_DIM, axis=1), xb
    xa = jnp.where(lo, x, 0.0)
    return xa, pltpu.roll(xa, HEAD_DIM, axis=1)


def _softmax_parts(parts, sink):
    tiles = [p[:, j * LANES:(j + 1) * LANES] for p in parts for j in range(p.shape[1] // LANES)]
    m = jnp.maximum(jnp.max(functools.reduce(jnp.maximum, tiles), axis=-1, keepdims=True), sink)
    return [jnp.exp(p - m) for p in parts], jnp.exp(sink - m)


def _with_ones(vab):
    r2 = vab.shape[0]
    lo = lax.broadcasted_iota(jnp.int32, (r2, LANES), 1) < HEAD_DIM
    top = lax.broadcasted_iota(jnp.int32, (r2, LANES), 0) < r2 // 2
    ones = jnp.where(lo == top, 1.0, 0.0).astype(vab.dtype)
    return jnp.concatenate([vab, ones], axis=1)


def _ctx_attn_kernel(sink_ref, q_ref, k_ref, v_ref, o_ref, nk_ref, nv_ref):
    t = q_ref.shape[1]
    k = k_ref[0]
    v = v_ref[0]
    lane_lo = lax.broadcasted_iota(jnp.int32, (t, LANES), 1) < HEAD_DIM
    for hk in range(KV_HEADS):
        nk_ref[0, hk] = k[:, hk * HEAD_DIM:(hk + 1) * HEAD_DIM]
        nv_ref[0, hk] = v[:, hk * HEAD_DIM:(hk + 1) * HEAD_DIM]
    for hk in range(KV_HEADS):
        cb = hk // 2
        ka, kb = _split_heads(k[:, cb * LANES:(cb + 1) * LANES], hk % 2)
        va, vb = _split_heads(v[:, cb * LANES:(cb + 1) * LANES], hk % 2)
        kab = jnp.concatenate([ka, kb], axis=0).astype(BF16)
        vab = _with_ones(jnp.concatenate([va, vb], axis=0).astype(BF16))
        for pp in range(2):
            p = 2 * hk + pp
            q2 = (q_ref[0, :, p * LANES:(p + 1) * LANES] * ATTN_SCALE).astype(BF16)
            s = _mm_nt(q2, kab)
            e0, x0 = _softmax_parts([s[:, :t]], sink_ref[hk, 2 * pp])
            e1, x1 = _softmax_parts([s[:, t:]], sink_ref[hk, 2 * pp + 1])
            o = _mm(jnp.concatenate([e0[0], e1[0]], axis=1), vab)
            o_ref[0, :, p * LANES:(p + 1) * LANES] = o[:, :LANES] / (o[:, LANES:] + jnp.where(lane_lo, x0, x1))


def _ctx_attention(sink, z3):
    bsz, t, _ = z3.shape
    return pl.pallas_call(
        _ctx_attn_kernel,
        grid=(bsz,),
        in_specs=[pl.BlockSpec(memory_space=pltpu.SMEM),
                  pl.BlockSpec((1, t, D_MODEL), lambda b: (b, 0, COL_Q // D_MODEL)),
                  pl.BlockSpec((1, t, KV_WIDTH), lambda b: (b, 0, COL_KB // KV_WIDTH)),
                  pl.BlockSpec((1, t, KV_WIDTH), lambda b: (b, 0, COL_VB // KV_WIDTH))],
        out_specs=[pl.BlockSpec((1, t, D_MODEL), lambda b: (b, 0, 0)),
                   pl.BlockSpec((1, KV_HEADS, t, HEAD_DIM), lambda b: (b, 0, 0, 0)),
                   pl.BlockSpec((1, KV_HEADS, t, HEAD_DIM), lambda b: (b, 0, 0, 0))],
        out_shape=[jax.ShapeDtypeStruct((bsz, t, D_MODEL), F32),
                   jax.ShapeDtypeStruct((bsz, KV_HEADS, t, HEAD_DIM), F32),
                   jax.ShapeDtypeStruct((bsz, KV_HEADS, t, HEAD_DIM), F32)],
        compiler_params=_cparams("arbitrary"),
        name="ctx_attn",
    )(sink, z3, z3, z3)


def _rope(x, cos, sin_signed):
    lane = lax.broadcasted_iota(jnp.int32, x.shape, 1)
    partner = jnp.where((lane & 31) < 16, pltpu.roll(x, LANES - 16, axis=1), pltpu.roll(x, 16, axis=1))
    return x * cos + partner * sin_signed


def _lat_attn_kernel(sink_ref, q_ref, k_ref, v_ref, ck_ref, cv_ref, cos_ref, sin_ref, o_ref, kab_s, vab_s):
    t = k_ref.shape[1]
    nb = t // QBLK
    i = pl.program_id(1)

    @pl.when(i == 0)
    def _():
        cos = cos_ref[...]
        sin = sin_ref[...]
        for cb in range(KV_HEADS // 2):
            kr = _rope(k_ref[0, :, cb * LANES:(cb + 1) * LANES], cos, sin)
            vv = v_ref[0, :, cb * LANES:(cb + 1) * LANES]
            for odd in range(2):
                hk = 2 * cb + odd
                ka, kb = _split_heads(kr, odd)
                va, vb = _split_heads(vv, odd)
                for j in range(nb):
                    rows = slice(j * QBLK, (j + 1) * QBLK)
                    kab_s[hk, j, 0:QBLK, :] = ka[rows].astype(BF16)
                    kab_s[hk, j, QBLK:2 * QBLK, :] = kb[rows].astype(BF16)
                    vab_s[hk, j] = _with_ones(jnp.concatenate([va[rows], vb[rows]], axis=0).astype(BF16))

    qrows = pl.ds(pl.multiple_of(i * QBLK, QBLK), QBLK)
    cos_q = cos_ref[qrows, :]
    sin_q = sin_ref[qrows, :]
    ri = lax.broadcasted_iota(jnp.int32, (QBLK, QBLK), 0)
    ci = lax.broadcasted_iota(jnp.int32, (QBLK, QBLK), 1)
    mask_prev = ci >= ri + jnp.where(i >= 1, 0, QBLK)
    mask_next = ci <= ri - jnp.where(i + 1 < nb, 0, QBLK)
    lane_lo = lax.broadcasted_iota(jnp.int32, (QBLK, LANES), 1) < HEAD_DIM
    jm = jnp.maximum(i - 1, 0)
    jp = jnp.minimum(i + 1, nb - 1)
    for hk in range(KV_HEADS):
        kblk = [kab_s[hk, jm], kab_s[hk, i], kab_s[hk, jp]]
        vblk = [vab_s[hk, jm], vab_s[hk, i], vab_s[hk, jp]]
        ckh = ck_ref[0, hk]
        cvh = cv_ref[0, hk]
        tc = ckh.shape[0] // 2
        for pp in range(2):
            p = 2 * hk + pp
            q2 = (_rope(q_ref[0, :, p * LANES:(p + 1) * LANES], cos_q, sin_q) * ATTN_SCALE).astype(BF16)
            s_loc = [_mm_nt(q2, kb) for kb in kblk]
            s_ctx = _mm_nt(q2, ckh)
            exps, invs = [], []
            for half in range(2):
                cols = slice(half * QBLK, (half + 1) * QBLK)
                parts = [jnp.where(mask_prev, s_loc[0][:, cols], NEG_INF),
                         s_loc[1][:, cols],
                         jnp.where(mask_next, s_loc[2][:, cols], NEG_INF),
                         s_ctx[:, half * tc:(half + 1) * tc]]
                es, sink_e = _softmax_parts(parts, sink_ref[hk, 2 * pp + half])
                exps.append(es)
                invs.append(sink_e)
            o = _mm(jnp.concatenate([exps[0][3], exps[1][3]], axis=1), cvh)
            for jj in range(3):
                o = o + _mm(jnp.concatenate([exps[0][jj], exps[1][jj]], axis=1), vblk[jj])
            o_ref[0, :, p * LANES:(p + 1) * LANES] = o[:, :LANES] / (o[:, LANES:] + jnp.where(lane_lo, invs[0], invs[1]))


def _lat_attention(sink, z3, ck2, cv2, cos_t, sin_t):
    bsz, t, _ = z3.shape
    nb = t // QBLK
    tc2 = ck2.shape[2]
    return pl.pallas_call(
        _lat_attn_kernel,
        grid=(bsz, nb),
        in_specs=[pl.BlockSpec(memory_space=pltpu.SMEM),
                  pl.BlockSpec((1, QBLK, D_MODEL), lambda b, i: (b, i, COL_Q // D_MODEL)),
                  pl.BlockSpec((1, t, KV_WIDTH), lambda b, i: (b, 0, COL_KB // KV_WIDTH)),
                  pl.BlockSpec((1, t, KV_WIDTH), lambda b, i: (b, 0, COL_VB // KV_WIDTH)),
                  pl.BlockSpec((1, KV_HEADS, tc2, LANES), lambda b, i: (b, 0, 0, 0)),
                  pl.BlockSpec((1, KV_HEADS, tc2, 2 * LANES), lambda b, i: (b, 0, 0, 0)),
                  pl.BlockSpec((t, LANES), lambda b, i: (0, 0)),
                  pl.BlockSpec((t, LANES), lambda b, i: (0, 0))],
        out_specs=pl.BlockSpec((1, QBLK, D_MODEL), lambda b, i: (b, i, 0)),
        out_shape=jax.ShapeDtypeStruct((bsz, t, D_MODEL), F32),
        scratch_shapes=[pltpu.VMEM((KV_HEADS, nb, 2 * QBLK, LANES), BF16),
                        pltpu.VMEM((KV_HEADS, nb, 2 * QBLK, 2 * LANES), BF16)],
        compiler_params=_cparams("arbitrary", "arbitrary"),
        name="lat_attn",
    )(sink, z3, z3, z3, ck2, cv2, cos_t, sin_t)


def _merge_kernel(x_ref, mod_ref, ya_ref, yb_ref, ga_ref, gb_ref, ma_ref, mb_ref,
                  woa_ref, wob_ref, wout_ref, fw_ref, o_ref):
    ga = ga_ref[...]
    gb = gb_ref[...]
    br_a = jnp.dot((ya_ref[...] * (ga * _sigmoid(ga))).astype(BF16), woa_ref[...], preferred_element_type=F32)
    br_b = jnp.dot((yb_ref[...] * (gb * _sigmoid(gb))).astype(BF16), wob_ref[...], preferred_element_type=F32)
    merged = _sigmoid(ma_ref[...]) * br_a + _sigmoid(mb_ref[...]) * br_b
    out = x_ref[...] + mod_ref[0, 2:3, :] * jnp.dot(merged.astype(BF16), wout_ref[...],
                                                     preferred_element_type=F32)
    o_ref[...] = out * lax.rsqrt(jnp.mean(out * out, axis=-1, keepdims=True) + RMS_EPS) * fw_ref[...]


def _merge(x2, mod3, ya2, yb2, z2, woa, wob, wout, fw, mod_row, tm):
    m = x2.shape[0]

    def tok(col=0):
        return pl.BlockSpec((tm, D_MODEL), lambda i, col=col: (i, col // D_MODEL))

    def wspec():
        return pl.BlockSpec((D_MODEL, D_MODEL), lambda i: (0, 0))

    return pl.pallas_call(
        _merge_kernel,
        grid=(m // tm,),
        in_specs=[tok(), pl.BlockSpec((1, 3, D_MODEL), lambda i: (mod_row(i), 0, 0)),
                  tok(), tok(), tok(COL_GA), tok(COL_GB), tok(COL_MA), tok(COL_MB),
                  wspec(), wspec(), wspec(), pl.BlockSpec((1, D_MODEL), lambda i: (0, 0))],
        out_specs=tok(),
        out_shape=jax.ShapeDtypeStruct((m, D_MODEL), F32),
        compiler_params=_cparams("arbitrary"),
        name="merge",
    )(x2, mod3, ya2, yb2, z2, z2, z2, z2, woa, wob, wout, fw.reshape(1, D_MODEL))


def _rope_tables(t):
    n_rows = t // GRID_W
    row = jnp.repeat(jnp.arange(n_rows), GRID_W)
    col = jnp.tile(jnp.arange(GRID_W), n_rows)
    nf = HEAD_DIM // 4
    inv = ROPE_BASE ** (-jnp.arange(nf, dtype=F32) / nf)
    ang_r = row.astype(F32)[:, None] * inv[None, :]
    ang_c = col.astype(F32)[:, None] * inv[None, :]
    cos = jnp.concatenate([jnp.cos(ang_r), jnp.cos(ang_r), jnp.cos(ang_c), jnp.cos(ang_c)], axis=-1)
    sin = jnp.concatenate([-jnp.sin(ang_r), jnp.sin(ang_r), -jnp.sin(ang_c), jnp.sin(ang_c)], axis=-1)
    return jnp.tile(cos, (1, 2)), jnp.tile(sin, (1, 2))


def _pad_heads(x, with_ones=False):
    zero = jnp.zeros_like(x)
    cols = [jnp.concatenate([x, zero], axis=-2), jnp.concatenate([zero, x], axis=-2)]
    if with_ones:
        one = jnp.ones_like(x)
        cols += [jnp.concatenate([one, zero], axis=-2), jnp.concatenate([zero, one], axis=-2)]
    return jnp.concatenate(cols, axis=-1).astype(BF16)


def kernel(x_prompt, x_sample, cache_k, cache_v, state_rwkv, c, c_ctx, norm_w, w_ada, b_ada, w_in, conv_a,
           w0, w_up, a0, a_up, k_k, k_a, r_k, ln_x_w, ln_x_b, w_oA, sink, w_oB, w_out, final_norm_w):
    depth = norm_w.shape[0]
    assert depth == 1
    bc, tc, _ = x_prompt.shape
    bd, td, _ = x_sample.shape
    l = 0
    a_cols = 3 * D_MODEL + 2 * LORA

    wi = w_in[l]
    o_ga = a_cols
    o_q = o_ga + D_MODEL
    o_kb = o_q + D_MODEL
    o_vb = o_kb + KV_WIDTH
    o_gb = o_vb + KV_WIDTH
    o_ma = o_gb + D_MODEL
    o_mb = o_ma + D_MODEL
    w_in_p = jnp.concatenate(
        [wi[:, 0:3 * D_MODEL], wi[:, o_ga:o_q], wi[:, o_q:o_kb], wi[:, o_gb:o_ma], wi[:, o_ma:o_mb],
         wi[:, o_mb:o_mb + D_MODEL], wi[:, o_kb:o_vb], wi[:, o_vb:o_gb], wi[:, 3 * D_MODEL:a_cols]],
        axis=1).astype(BF16)
    conv_rkv = conv_a[l][:, 0:3 * D_MODEL]
    conv_wa = conv_a[l][:, 3 * D_MODEL:a_cols]
    pch = jnp.concatenate(
        [k_k[l][None], k_a[l][None], r_k[l].reshape(1, D_MODEL), ln_x_w[l][None], ln_x_b[l][None],
         0.5 * w0[l], 0.5 * a0[l], jnp.zeros((7, D_MODEL), F32)], axis=0)
    wup = w_up[l].reshape(2, LORA, N_PAIRS, LANES).transpose(0, 2, 1, 3)
    aup = a_up[l].reshape(2, LORA, N_PAIRS, LANES).transpose(0, 2, 1, 3)
    zl = jnp.zeros_like(wup)
    lora = jnp.concatenate([jnp.concatenate([wup, zl], axis=-1),
                            jnp.concatenate([zl, aup], axis=-1)], axis=-2)
    lora = (0.5 * lora).astype(BF16)
    woa, wob, wout = w_oA[l].astype(BF16), w_oB[l].astype(BF16), w_out[l].astype(BF16)
    sink_l = sink[l]

    cond8 = jnp.concatenate([c_ctx[None], c, jnp.zeros((8 - 1 - bd, D_MODEL), F32)], axis=0)
    mod3 = _modulation(cond8, w_ada[l], b_ada[l]).reshape(8, 3, D_MODEL)

    tm = 512
    ctx_row = lambda i: 0
    lat_row = lambda i: 1 + (i * tm) // td

    xp2 = x_prompt.reshape(bc * tc, D_MODEL)
    zc = _inproj(xp2, mod3, norm_w[l], w_in_p, ctx_row, tm)
    zc3 = zc.reshape(bc, tc, ZW)
    ya_c, st_c = _rwkv(zc3, conv_rkv, conv_wa, pch, lora, None, PAIRS_CTX)
    yb_c, new_k, new_v = _ctx_attention(sink_l, zc3)
    y_prompt = _merge(xp2, mod3, ya_c.reshape(bc * tc, D_MODEL), yb_c.reshape(bc * tc, D_MODEL), zc,
                      woa, wob, wout, final_norm_w, ctx_row, tm).reshape(bc, tc, D_MODEL)
    new_state = st_c[:, None]

    xs2 = x_sample.reshape(bd * td, D_MODEL)
    zd = _inproj(xs2, mod3, norm_w[l], w_in_p, lat_row, tm)
    zd3 = zd.reshape(bd, td, ZW)
    s0p = state_rwkv[:, l].reshape(bd, 2, N_PAIRS, 2, HEAD_DIM, HEAD_DIM).transpose(0, 1, 2, 4, 3, 5)
    s0p = s0p.reshape(bd, 2, N_PAIRS, HEAD_DIM, LANES)
    ya_d, _ = _rwkv(zd3, conv_rkv, conv_wa, pch, lora, s0p, PAIRS_LAT)
    cos_t, sin_t = _rope_tables(td)
    yb_d = _lat_attention(sink_l, zd3, _pad_heads(cache_k[:, l]), _pad_heads(cache_v[:, l], with_ones=True), cos_t, sin_t)
    y_sample = _merge(xs2, mod3, ya_d.reshape(bd * td, D_MODEL), yb_d.reshape(bd * td, D_MODEL), zd,
                      woa, wob, wout, final_norm_w, lat_row, tm).reshape(bd, td, D_MODEL)

    return (y_prompt, y_sample, new_k[:, None], new_v[:, None], new_state)
```

```python
import functools
import math

import jax
import jax.numpy as jnp
from jax import lax
from jax.experimental import pallas as pl
from jax.experimental.pallas import tpu as pltpu

F32 = jnp.float32
BF16 = jnp.bfloat16

D_MODEL = 1024
HEAD_DIM = 64
LANES = 128
N_PAIRS = D_MODEL // LANES
KV_HEADS = 4
KV_WIDTH = KV_HEADS * HEAD_DIM
LORA = 64
GRID_W = 64
WINDOW = 128
QBLK = 128
ROPE_BASE = 10000.0
RMS_EPS = 1e-6
GN_EPS = 64e-5
NEG_INF = -1e30
ATTN_SCALE = HEAD_DIM ** -0.5
PAIRS_CTX, PAIRS_LAT = 8, 2
GROUP = 4
HALF_DECAY_SCALE = -0.5 * math.exp(-0.5)
CHUNK = 64

COL_R, COL_K, COL_V, COL_GA, COL_Q, COL_GB, COL_MA, COL_MB = (i * D_MODEL for i in range(8))
COL_KB = 8 * D_MODEL
COL_VB = COL_KB + KV_WIDTH
COL_WA = COL_VB + KV_WIDTH
ZW = COL_WA + 2 * LORA
TN_IN = 2944

VMEM_LIMIT = 56 * 1024 * 1024


def _cparams(*sem):
    return pltpu.CompilerParams(dimension_semantics=sem, vmem_limit_bytes=VMEM_LIMIT)


def _sigmoid(x):
    return 0.5 * jnp.tanh(0.5 * x) + 0.5


def _mm(a, b):
    return jnp.dot(a.astype(BF16), b.astype(BF16), preferred_element_type=F32)


def _mm_nt(a, b):
    return lax.dot_general(a.astype(BF16), b.astype(BF16), (((1,), (1,)), ((), ())),
                           preferred_element_type=F32)


def _mod_kernel(c_ref, w_ref, b_ref, o_ref):
    c = c_ref[...]
    o_ref[...] = jnp.dot(c * _sigmoid(c), w_ref[...], preferred_element_type=F32) + b_ref[...]


def _modulation(cond8, w_ada, b_ada):
    n = w_ada.shape[1]
    tn = D_MODEL
    return pl.pallas_call(
        _mod_kernel,
        grid=(n // tn,),
        in_specs=[pl.BlockSpec((8, D_MODEL), lambda j: (0, 0)),
                  pl.BlockSpec((D_MODEL, tn), lambda j: (0, j)),
                  pl.BlockSpec((1, tn), lambda j: (0, j))],
        out_specs=pl.BlockSpec((8, tn), lambda j: (0, j)),
        out_shape=jax.ShapeDtypeStruct((8, n), F32),
        compiler_params=_cparams("arbitrary"),
        name="mod",
    )(cond8, w_ada, b_ada.reshape(1, n))


def _inproj_kernel(x_ref, mod_ref, nw_ref, w_ref, o_ref):
    x = x_ref[...]
    y = x * lax.rsqrt(jnp.mean(x * x, axis=-1, keepdims=True) + RMS_EPS) * nw_ref[...]
    h = y * (1.0 + mod_ref[0, 1:2, :]) + mod_ref[0, 0:1, :]
    o_ref[...] = jnp.dot(h.astype(BF16), w_ref[...], preferred_element_type=F32)


def _inproj(x2, mod3, norm_w, w_in_p, mod_row, tm):
    m = x2.shape[0]
    return pl.pallas_call(
        _inproj_kernel,
        grid=(ZW // TN_IN, m // tm),
        in_specs=[pl.BlockSpec((tm, D_MODEL), lambda n, i: (i, 0)),
                  pl.BlockSpec((1, 3, D_MODEL), lambda n, i: (mod_row(i), 0, 0)),
                  pl.BlockSpec((1, D_MODEL), lambda n, i: (0, 0)),
                  pl.BlockSpec((D_MODEL, TN_IN), lambda n, i: (0, n))],
        out_specs=pl.BlockSpec((tm, TN_IN), lambda n, i: (i, n)),
        out_shape=jax.ShapeDtypeStruct((m, ZW), F32),
        compiler_params=_cparams("arbitrary", "arbitrary"),
        name="inproj",
    )(x2, mod3, norm_w.reshape(1, D_MODEL), w_in_p)


def _segsum(x, e_bd):
    return jnp.dot(x.astype(BF16), e_bd, preferred_element_type=F32)


def _bmm(a, b):
    return jnp.einsum("bij,bjk->bik", a.astype(BF16), b.astype(BF16), preferred_element_type=F32)


def _bmm_nt(a, b):
    return jnp.einsum("bik,bjk->bij", a.astype(BF16), b.astype(BF16), preferred_element_type=F32)


def _bmm_tn(a, b):
    return jnp.einsum("bki,bkj->bij", a.astype(BF16), b.astype(BF16), preferred_element_type=F32)


def _chunk_local(r, kd, v, a, b, lw):
    c = CHUNK
    n = r.shape[0]
    g = n // 2

    def by_dir(fwd, bwd):
        return jnp.concatenate([fwd, bwd], axis=0)

    def masked(mask_f, mask_r, x):
        return by_dir(jnp.where(mask_f, x[:g], 0.0), jnp.where(mask_r, x[g:], 0.0))

    ti = lax.broadcasted_iota(jnp.int32, (c, c), 0)
    tj = lax.broadcasted_iota(jnp.int32, (c, c), 1)
    tri = by_dir(jnp.broadcast_to(jnp.where(tj <= ti, 1.0, 0.0).astype(BF16), (g, c, c)),
                 jnp.broadcast_to(jnp.where(tj >= ti, 1.0, 0.0).astype(BF16), (g, c, c)))
    lw_hi = lw.astype(BF16)
    lw_lo = (lw - lw_hi.astype(F32)).astype(BF16)
    cs = _bmm(tri, lw_hi) + _bmm(tri, lw_lo)
    tot = by_dir(cs[:g, c - 1:c], cs[g:, 0:1])
    e_in = jnp.exp(cs)
    e_neg = jnp.exp(-cs)
    pc = jnp.exp(tot)
    e_end = pc * e_neg
    at = a * jnp.exp(cs - lw)
    rt = r * e_in
    bt = b * e_neg
    kt = kd * e_neg
    bh = b * e_end
    kh = kd * e_end

    h0 = lax.broadcasted_iota(jnp.int32, (c, LANES), 1) < HEAD_DIM

    def stack2(x):
        m = h0 if x.shape[2] == LANES else jnp.concatenate([h0, h0], axis=1)
        xb = x.astype(BF16)
        zero = jnp.zeros_like(xb)
        return jnp.concatenate([jnp.where(m, xb, zero), jnp.where(m, zero, xb)], axis=1)

    def mul(x, y):
        return _bmm(x, stack2(y))

    gm = _bmm_nt(jnp.concatenate([at, rt], axis=1), jnp.concatenate([stack2(bt), stack2(kt)], axis=1))

    ri = lax.broadcasted_iota(jnp.int32, (c, LANES), 0)
    ci = lax.broadcasted_iota(jnp.int32, (c, LANES), 1) & (c - 1)
    diag = ri == ci

    a_ab = masked(ci < ri, ci > ri, gm[:, 0:c, 0:LANES])
    a_ak = masked(ci < ri, ci > ri, gm[:, 0:c, LANES:])
    a_rb = masked(ci <= ri, ci >= ri, gm[:, c:, 0:LANES])
    a_rk = masked(ci <= ri, ci >= ri, gm[:, c:, LANES:])

    def blk(shift):
        return (ri >> shift) == (ci >> shift)

    eye = jnp.where(diag, 1.0, 0.0)
    a8 = jnp.where(blk(3), a_ab, 0.0)
    a8_2 = mul(a8, a8)
    a8_4 = mul(a8_2, a8_2)
    tm = mul(mul(eye + a8, eye + a8_2), eye + a8_4)
    for sh in (3, 4, 5):
        off = jnp.where(blk(sh + 1) & jnp.logical_not(blk(sh)), a_ab, 0.0)
        tm2 = stack2(tm)
        tm = tm + _bmm(mul(tm, off), tm2)

    v2 = stack2(v)
    wu = _bmm(tm, jnp.concatenate([stack2(at), stack2(mul(a_ak, v))], axis=2))
    wu2 = stack2(wu)
    qy = _bmm(a_rb, wu2)
    qeff = rt + qy[:, :, :LANES]
    yloc = qy[:, :, LANES:] + _bmm(a_rk, v2)
    lhs_t = jnp.concatenate([jnp.concatenate([bh, pltpu.roll(bh, HEAD_DIM, axis=2)], axis=1),
                             jnp.concatenate([kh, pltpu.roll(kh, HEAD_DIM, axis=2)], axis=1)], axis=1)
    rhs = jnp.concatenate([wu2, jnp.concatenate([jnp.zeros_like(v2), v2], axis=2)], axis=1)
    md = _bmm_tn(lhs_t, rhs)[:, :HEAD_DIM]
    mt = md[:, :, :LANES] + jnp.where(diag, pc, 0.0)
    dt = md[:, :, LANES:]
    return qeff, yloc, mt, dt


def _rwkv_kernel(*refs, has_s0, pw):
    if has_s0:
        (r_ref, k_ref, v_ref, wa_ref, cr_ref, ck_ref, cv_ref, cwa_ref, pch_ref, lora_ref, s0_ref,
         ya_ref, so_ref, r_s, v_s, a_s, bon_s, kd_s, b_s, lw_s, y_s, q_s, mt_s, dt_s, z_s) = refs
    else:
        (r_ref, k_ref, v_ref, wa_ref, cr_ref, ck_ref, cv_ref, cwa_ref, pch_ref, lora_ref,
         ya_ref, so_ref, r_s, v_s, a_s, bon_s, kd_s, b_s, lw_s, y_s, q_s, mt_s, dt_s, z_s) = refs
        s0_ref = None
    t = r_ref.shape[1]
    nc = t // CHUNK
    h0 = lax.broadcasted_iota(jnp.int32, (t, LANES), 1) < HEAD_DIM
    ri = lax.broadcasted_iota(jnp.int32, (LANES, LANES), 0)
    ci = lax.broadcasted_iota(jnp.int32, (LANES, LANES), 1)
    e_bd = jnp.where((ri >> 6) == (ci >> 6), 1.0, 0.0).astype(BF16)

    row8 = lax.broadcasted_iota(jnp.int32, (8, LANES), 0)

    def conv(x, cw):
        prev = pltpu.roll(x, 1, axis=0)
        nxt = pltpu.roll(x, t - 1, axis=0)

        def taps(p, c, n):
            return cw[0:1, :] * p + cw[1:2, :] * c + cw[2:3, :] * n

        first = taps(jnp.where(row8 == 0, 0.0, prev[0:8]), x[0:8], nxt[0:8])
        last = taps(prev[t - 8:t], x[t - 8:t], jnp.where(row8 == 7, 0.0, nxt[t - 8:t]))
        mid = taps(prev[8:t - 8], x[8:t - 8], nxt[8:t - 8])
        return jnp.concatenate([first, mid, last], axis=0)

    wa = conv(wa_ref[0], cwa_ref[...])
    lora_in = jnp.where(h0, jnp.tanh(wa), wa).astype(BF16)
    for p in range(pw):
        ls = slice(p * LANES, (p + 1) * LANES)
        pch = pch_ref[:, ls]
        r = conv(r_ref[0, :, ls], cr_ref[:, ls])
        k = conv(k_ref[0, :, ls], ck_ref[:, ls])
        v = conv(v_ref[0, :, ls], cv_ref[:, ls])
        k_k, k_a, r_k = pch[0:1], pch[1:2], pch[2:3]
        kk = k * k_k
        kk = kk * lax.rsqrt(jnp.maximum(_segsum(kk * kk, e_bd), 1e-12))
        r_s[p] = r
        v_s[p] = v
        a_s[p] = -kk
        bon_s[p] = _segsum(r * k * r_k, e_bd) * v
        half_kk = 0.5 * kk
        for d in range(2):
            lo = jnp.dot(lora_in, lora_ref[d, p], preferred_element_type=F32)
            th_w = jnp.tanh(pch[5 + d:6 + d] + lo[:, :LANES])
            th_a = jnp.tanh(pch[7 + d:8 + d] + lo[:, LANES:])
            lw_s[p, d] = HALF_DECAY_SCALE * th_w + HALF_DECAY_SCALE
            kd_s[p, d] = k * ((1.0 - 0.5 * k_a) + (0.5 * k_a) * th_a)
            b_s[p, d] = half_kk * th_a + half_kk

    gsz = GROUP * CHUNK

    def local_body(g, carry):
        rows = pl.ds(pl.multiple_of(g * gsz, gsz), gsz)
        cidx = pl.ds(pl.multiple_of(g * GROUP, GROUP), GROUP)

        def shared(ref):
            x = jnp.concatenate([ref[p, rows, :].reshape(GROUP, CHUNK, LANES) for p in range(pw)], axis=0)
            return jnp.concatenate([x, x], axis=0)

        def per_dir(ref):
            return jnp.concatenate([ref[p, d, rows, :].reshape(GROUP, CHUNK, LANES)
                                    for d in range(2) for p in range(pw)], axis=0)

        qeff, yloc, mt, dt = _chunk_local(shared(r_s), per_dir(kd_s), shared(v_s), shared(a_s),
                                          per_dir(b_s), per_dir(lw_s))
        for d in range(2):
            for p in range(pw):
                part = slice((d * pw + p) * GROUP, (d * pw + p + 1) * GROUP)
                q_s[p, d, rows, :] = qeff[part].reshape(gsz, LANES)
                y_s[p, d, rows, :] = yloc[part].reshape(gsz, LANES)
                mt_s[p, d, cidx] = mt[part]
                dt_s[p, d, cidx] = dt[part]
        return carry

    lax.fori_loop(0, nc // GROUP, local_body, 0)

    hl = lax.broadcasted_iota(jnp.int32, (HEAD_DIM, LANES), 1) < HEAD_DIM

    def stack2(x):
        return jnp.concatenate([jnp.where(hl, x, 0.0), jnp.where(hl, 0.0, x)], axis=0)

    for p in range(pw):
        for d in range(2):
            if s0_ref is None:
                z_s[p, d] = jnp.zeros((HEAD_DIM, LANES), F32)
            else:
                zt = stack2(s0_ref[0, d, p]).T
                z_s[p, d] = zt[:HEAD_DIM] + zt[HEAD_DIM:]

    def seq_body(i, carry):
        for p in range(pw):
            for d in range(2):
                c = i if d == 0 else nc - 1 - i
                sl = pl.ds(pl.multiple_of(c * CHUNK, CHUNK), CHUNK)
                zb = stack2(z_s[p, d]).astype(BF16)
                y_s[p, d, sl, :] = y_s[p, d, sl, :] + jnp.dot(q_s[p, d, sl, :].astype(BF16), zb,
                                                              preferred_element_type=F32)
                z_s[p, d] = jnp.dot(mt_s[p, d, c].astype(BF16), zb, preferred_element_type=F32) + dt_s[p, d, c]
        return carry

    lax.fori_loop(0, nc, seq_body, 0)

    for p in range(pw):
        ls = slice(p * LANES, (p + 1) * LANES)
        for d in range(2):
            zt = stack2(z_s[p, d]).T
            so_ref[0, d, 2 * p] = zt[:HEAD_DIM, :HEAD_DIM]
            so_ref[0, d, 2 * p + 1] = zt[HEAD_DIM:, HEAD_DIM:]
        y = y_s[p, 0] + y_s[p, 1]
        mu = _segsum(y, e_bd) * (1.0 / HEAD_DIM)
        yc = y - mu
        var = _segsum(yc * yc, e_bd) * (1.0 / HEAD_DIM)
        ya_ref[0, :, ls] = (yc * lax.rsqrt(var + GN_EPS) * pch_ref[3:4, ls] + pch_ref[4:5, ls] + bon_s[p])


def _rwkv(z3, conv_rkv, conv_wa, pch, lora, s0p, pw):
    bsz, t, _ = z3.shape
    nc = t // CHUNK
    wl = pw * LANES
    has_s0 = s0p is not None

    def zspec(col):
        return pl.BlockSpec((1, t, wl), lambda b, j, col=col: (b, 0, col // wl + j))

    def cspec(col):
        return pl.BlockSpec((3, wl), lambda b, j, col=col: (0, col // wl + j))

    in_specs = [zspec(COL_R), zspec(COL_K), zspec(COL_V),
                pl.BlockSpec((1, t, LANES), lambda b, j: (b, 0, COL_WA // LANES)),
                cspec(COL_R), cspec(COL_K), cspec(COL_V),
                pl.BlockSpec((3, LANES), lambda b, j: (0, 0)),
                pl.BlockSpec((16, wl), lambda b, j: (0, j)),
                pl.BlockSpec((2, pw, LANES, 2 * LANES), lambda b, j: (0, j, 0, 0))]
    args = [z3, z3, z3, z3, conv_rkv, conv_rkv, conv_rkv, conv_wa, pch, lora]
    if has_s0:
        in_specs.append(pl.BlockSpec((1, 2, pw, HEAD_DIM, LANES), lambda b, j: (b, 0, j, 0, 0)))
        args.append(s0p)
    tl = (t, LANES)
    scratch = [pltpu.VMEM((pw,) + tl, F32)] * 4 + [pltpu.VMEM((pw, 2) + tl, F32)] * 5 \
        + [pltpu.VMEM((pw, 2, nc, HEAD_DIM, LANES), F32)] * 2 + [pltpu.VMEM((pw, 2, HEAD_DIM, LANES), F32)]
    return pl.pallas_call(
        functools.partial(_rwkv_kernel, has_s0=has_s0, pw=pw),
        grid=(bsz, N_PAIRS // pw),
        in_specs=in_specs,
        out_specs=[pl.BlockSpec((1, t, wl), lambda b, j: (b, 0, j)),
                   pl.BlockSpec((1, 2, 2 * pw, HEAD_DIM, HEAD_DIM), lambda b, j: (b, 0, j, 0, 0))],
        out_shape=[jax.ShapeDtypeStruct((bsz, t, D_MODEL), F32),
                   jax.ShapeDtypeStruct((bsz, 2, 2 * N_PAIRS, HEAD_DIM, HEAD_DIM), F32)],
        scratch_shapes=scratch,
        compiler_params=_cparams("arbitrary", "arbitrary"),
        name="rwkv_s0" if has_s0 else "rwkv",
    )(*args)


def _split_heads(x, odd):
    lo = lax.broadcasted_iota(jnp.int32, x.shape, 1) < HEAD_DIM
    if odd:
        xb = jnp.where(lo, 0.0, x)
        return pltpu.roll(xb, HEAD_DIM, axis=1), xb
    xa = jnp.where(lo, x, 0.0)
    return xa, pltpu.roll(xa, HEAD_DIM, axis=1)


def _softmax_parts(parts, sink):
    tiles = [p[:, j * LANES:(j + 1) * LANES] for p in parts for j in range(p.shape[1] // LANES)]
    m = jnp.maximum(jnp.max(functools.reduce(jnp.maximum, tiles), axis=-1, keepdims=True), sink)
    return [jnp.exp(p - m) for p in parts], jnp.exp(sink - m)


def _with_ones(vab):
    r2 = vab.shape[0]
    lo = lax.broadcasted_iota(jnp.int32, (r2, LANES), 1) < HEAD_DIM
    top = lax.broadcasted_iota(jnp.int32, (r2, LANES), 0) < r2 // 2
    ones = jnp.where(lo == top, 1.0, 0.0).astype(vab.dtype)
    return jnp.concatenate([vab, ones], axis=1)


def _ctx_attn_kernel(sink_ref, q_ref, k_ref, v_ref, o_ref, nk_ref, nv_ref):
    t = q_ref.shape[1]
    k = k_ref[0]
    v = v_ref[0]
    lane_lo = lax.broadcasted_iota(jnp.int32, (t, LANES), 1) < HEAD_DIM
    for hk in range(KV_HEADS):
        nk_ref[0, hk] = k[:, hk * HEAD_DIM:(hk + 1) * HEAD_DIM]
        nv_ref[0, hk] = v[:, hk * HEAD_DIM:(hk + 1) * HEAD_DIM]
    for hk in range(KV_HEADS):
        cb = hk // 2
        ka, kb = _split_heads(k[:, cb * LANES:(cb + 1) * LANES], hk % 2)
        va, vb = _split_heads(v[:, cb * LANES:(cb + 1) * LANES], hk % 2)
        kab = jnp.concatenate([ka, kb], axis=0).astype(BF16)
        vab = _with_ones(jnp.concatenate([va, vb], axis=0).astype(BF16))
        for pp in range(2):
            p = 2 * hk + pp
            q2 = (q_ref[0, :, p * LANES:(p + 1) * LANES] * ATTN_SCALE).astype(BF16)
            s = _mm_nt(q2, kab)
            e0, x0 = _softmax_parts([s[:, :t]], sink_ref[hk, 2 * pp])
            e1, x1 = _softmax_parts([s[:, t:]], sink_ref[hk, 2 * pp + 1])
            o = _mm(jnp.concatenate([e0[0], e1[0]], axis=1), vab)
            o_ref[0, :, p * LANES:(p + 1) * LANES] = o[:, :LANES] / (o[:, LANES:] + jnp.where(lane_lo, x0, x1))


def _ctx_attention(sink, z3):
    bsz, t, _ = z3.shape
    return pl.pallas_call(
        _ctx_attn_kernel,
        grid=(bsz,),
        in_specs=[pl.BlockSpec(memory_space=pltpu.SMEM),
                  pl.BlockSpec((1, t, D_MODEL), lambda b: (b, 0, COL_Q // D_MODEL)),
                  pl.BlockSpec((1, t, KV_WIDTH), lambda b: (b, 0, COL_KB // KV_WIDTH)),
                  pl.BlockSpec((1, t, KV_WIDTH), lambda b: (b, 0, COL_VB // KV_WIDTH))],
        out_specs=[pl.BlockSpec((1, t, D_MODEL), lambda b: (b, 0, 0)),
                   pl.BlockSpec((1, KV_HEADS, t, HEAD_DIM), lambda b: (b, 0, 0, 0)),
                   pl.BlockSpec((1, KV_HEADS, t, HEAD_DIM), lambda b: (b, 0, 0, 0))],
        out_shape=[jax.ShapeDtypeStruct((bsz, t, D_MODEL), F32),
                   jax.ShapeDtypeStruct((bsz, KV_HEADS, t, HEAD_DIM), F32),
                   jax.ShapeDtypeStruct((bsz, KV_HEADS, t, HEAD_DIM), F32)],
        compiler_params=_cparams("arbitrary"),
        name="ctx_attn",
    )(sink, z3, z3, z3)


def _rope(x, cos, sin_signed):
    lane = lax.broadcasted_iota(jnp.int32, x.shape, 1)
    partner = jnp.where((lane & 31) < 16, pltpu.roll(x, LANES - 16, axis=1), pltpu.roll(x, 16, axis=1))
    return x * cos + partner * sin_signed


def _lat_attn_kernel(sink_ref, q_ref, k_ref, v_ref, ck_ref, cv_ref, cos_ref, sin_ref, o_ref, kab_s, vab_s):
    t = k_ref.shape[1]
    nb = t // QBLK
    i = pl.program_id(1)

    @pl.when(i == 0)
    def _():
        cos = cos_ref[...]
        sin = sin_ref[...]
        for cb in range(KV_HEADS // 2):
            kr = _rope(k_ref[0, :, cb * LANES:(cb + 1) * LANES], cos, sin)
            vv = v_ref[0, :, cb * LANES:(cb + 1) * LANES]
            for odd in range(2):
                hk = 2 * cb + odd
                ka, kb = _split_heads(kr, odd)
                va, vb = _split_heads(vv, odd)
                for j in range(nb):
                    rows = slice(j * QBLK, (j + 1) * QBLK)
                    kab_s[hk, j, 0:QBLK, :] = ka[rows].astype(BF16)
                    kab_s[hk, j, QBLK:2 * QBLK, :] = kb[rows].astype(BF16)
                    vab_s[hk, j] = _with_ones(jnp.concatenate([va[rows], vb[rows]], axis=0).astype(BF16))

    qrows = pl.ds(pl.multiple_of(i * QBLK, QBLK), QBLK)
    cos_q = cos_ref[qrows, :]
    sin_q = sin_ref[qrows, :]
    ri = lax.broadcasted_iota(jnp.int32, (QBLK, QBLK), 0)
    ci = lax.broadcasted_iota(jnp.int32, (QBLK, QBLK), 1)
    mask_prev = ci >= ri + jnp.where(i >= 1, 0, QBLK)
    mask_next = ci <= ri - jnp.where(i + 1 < nb, 0, QBLK)
    lane_lo = lax.broadcasted_iota(jnp.int32, (QBLK, LANES), 1) < HEAD_DIM
    jm = jnp.maximum(i - 1, 0)
    jp = jnp.minimum(i + 1, nb - 1)
    for hk in range(KV_HEADS):
        kblk = [kab_s[hk, jm], kab_s[hk, i], kab_s[hk, jp]]
        vblk = [vab_s[hk, jm], vab_s[hk, i], vab_s[hk, jp]]
        ckh = ck_ref[0, hk]
        cvh = cv_ref[0, hk]
        tc = ckh.shape[0] // 2
        for pp in range(2):
            p = 2 * hk + pp
            q2 = (_rope(q_ref[0, :, p * LANES:(p + 1) * LANES], cos_q, sin_q) * ATTN_SCALE).astype(BF16)
            s_loc = [_mm_nt(q2, kb) for kb in kblk]
            s_ctx = _mm_nt(q2, ckh)
            exps, invs = [], []
            for half in range(2):
                cols = slice(half * QBLK, (half + 1) * QBLK)
                parts = [jnp.where(mask_prev, s_loc[0][:, cols], NEG_INF),
                         s_loc[1][:, cols],
                         jnp.where(mask_next, s_loc[2][:, cols], NEG_INF),
                         s_ctx[:, half * tc:(half + 1) * tc]]
                es, sink_e = _softmax_parts(parts, sink_ref[hk, 2 * pp + half])
                exps.append(es)
                invs.append(sink_e)
            o = _mm(jnp.concatenate([exps[0][3], exps[1][3]], axis=1), cvh)
            for jj in range(3):
                o = o + _mm(jnp.concatenate([exps[0][jj], exps[1][jj]], axis=1), vblk[jj])
            o_ref[0, :, p * LANES:(p + 1) * LANES] = o[:, :LANES] / (o[:, LANES:] + jnp.where(lane_lo, invs[0], invs[1]))


def _lat_attention(sink, z3, ck2, cv2, cos_t, sin_t):
    bsz, t, _ = z3.shape
    nb = t // QBLK
    tc2 = ck2.shape[2]
    return pl.pallas_call(
        _lat_attn_kernel,
        grid=(bsz, nb),
        in_specs=[pl.BlockSpec(memory_space=pltpu.SMEM),
                  pl.BlockSpec((1, QBLK, D_MODEL), lambda b, i: (b, i, COL_Q // D_MODEL)),
                  pl.BlockSpec((1, t, KV_WIDTH), lambda b, i: (b, 0, COL_KB // KV_WIDTH)),
                  pl.BlockSpec((1, t, KV_WIDTH), lambda b, i: (b, 0, COL_VB // KV_WIDTH)),
                  pl.BlockSpec((1, KV_HEADS, tc2, LANES), lambda b, i: (b, 0, 0, 0)),
                  pl.BlockSpec((1, KV_HEADS, tc2, 2 * LANES), lambda b, i: (b, 0, 0, 0)),
                  pl.BlockSpec((t, LANES), lambda b, i: (0, 0)),
                  pl.BlockSpec((t, LANES), lambda b, i: (0, 0))],
        out_specs=pl.BlockSpec((1, QBLK, D_MODEL), lambda b, i: (b, i, 0)),
        out_shape=jax.ShapeDtypeStruct((bsz, t, D_MODEL), F32),
        scratch_shapes=[pltpu.VMEM((KV_HEADS, nb, 2 * QBLK, LANES), BF16),
                        pltpu.VMEM((KV_HEADS, nb, 2 * QBLK, 2 * LANES), BF16)],
        compiler_params=_cparams("arbitrary", "arbitrary"),
        name="lat_attn",
    )(sink, z3, z3, z3, ck2, cv2, cos_t, sin_t)


def _merge_kernel(x_ref, mod_ref, ya_ref, yb_ref, ga_ref, gb_ref, ma_ref, mb_ref,
                  woa_ref, wob_ref, wout_ref, fw_ref, o_ref):
    ga = ga_ref[...]
    gb = gb_ref[...]
    br_a = jnp.dot((ya_ref[...] * (ga * _sigmoid(ga))).astype(BF16), woa_ref[...], preferred_element_type=F32)
    br_b = jnp.dot((yb_ref[...] * (gb * _sigmoid(gb))).astype(BF16), wob_ref[...], preferred_element_type=F32)
    merged = _sigmoid(ma_ref[...]) * br_a + _sigmoid(mb_ref[...]) * br_b
    out = x_ref[...] + mod_ref[0, 2:3, :] * jnp.dot(merged.astype(BF16), wout_ref[...],
                                                     preferred_element_type=F32)
    o_ref[...] = out * lax.rsqrt(jnp.mean(out * out, axis=-1, keepdims=True) + RMS_EPS) * fw_ref[...]


def _merge(x2, mod3, ya2, yb2, z2, woa, wob, wout, fw, mod_row, tm):
    m = x2.shape[0]

    def tok(col=0):
        return pl.BlockSpec((tm, D_MODEL), lambda i, col=col: (i, col // D_MODEL))

    def wspec():
        return pl.BlockSpec((D_MODEL, D_MODEL), lambda i: (0, 0))

    return pl.pallas_call(
        _merge_kernel,
        grid=(m // tm,),
        in_specs=[tok(), pl.BlockSpec((1, 3, D_MODEL), lambda i: (mod_row(i), 0, 0)),
                  tok(), tok(), tok(COL_GA), tok(COL_GB), tok(COL_MA), tok(COL_MB),
                  wspec(), wspec(), wspec(), pl.BlockSpec((1, D_MODEL), lambda i: (0, 0))],
        out_specs=tok(),
        out_shape=jax.ShapeDtypeStruct((m, D_MODEL), F32),
        compiler_params=_cparams("arbitrary"),
        name="merge",
    )(x2, mod3, ya2, yb2, z2, z2, z2, z2, woa, wob, wout, fw.reshape(1, D_MODEL))


def _rope_tables(t):
    n_rows = t // GRID_W
    row = jnp.repeat(jnp.arange(n_rows), GRID_W)
    col = jnp.tile(jnp.arange(GRID_W), n_rows)
    nf = HEAD_DIM // 4
    inv = ROPE_BASE ** (-jnp.arange(nf, dtype=F32) / nf)
    ang_r = row.astype(F32)[:, None] * inv[None, :]
    ang_c = col.astype(F32)[:, None] * inv[None, :]
    cos = jnp.concatenate([jnp.cos(ang_r), jnp.cos(ang_r), jnp.cos(ang_c), jnp.cos(ang_c)], axis=-1)
    sin = jnp.concatenate([-jnp.sin(ang_r), jnp.sin(ang_r), -jnp.sin(ang_c), jnp.sin(ang_c)], axis=-1)
    return jnp.tile(cos, (1, 2)), jnp.tile(sin, (1, 2))


def _pad_heads(x, with_ones=False):
    zero = jnp.zeros_like(x)
    cols = [jnp.concatenate([x, zero], axis=-2), jnp.concatenate([zero, x], axis=-2)]
    if with_ones:
        one = jnp.ones_like(x)
        cols += [jnp.concatenate([one, zero], axis=-2), jnp.concatenate([zero, one], axis=-2)]
    return jnp.concatenate(cols, axis=-1).astype(BF16)


def kernel(x_prompt, x_sample, cache_k, cache_v, state_rwkv, c, c_ctx, norm_w, w_ada, b_ada, w_in, conv_a,
           w0, w_up, a0, a_up, k_k, k_a, r_k, ln_x_w, ln_x_b, w_oA, sink, w_oB, w_out, final_norm_w):
    depth = norm_w.shape[0]
    assert depth == 1
    bc, tc, _ = x_prompt.shape
    bd, td, _ = x_sample.shape
    l = 0
    a_cols = 3 * D_MODEL + 2 * LORA

    wi = w_in[l]
    o_ga = a_cols
    o_q = o_ga + D_MODEL
    o_kb = o_q + D_MODEL
    o_vb = o_kb + KV_WIDTH
    o_gb = o_vb + KV_WIDTH
    o_ma = o_gb + D_MODEL
    o_mb = o_ma + D_MODEL
    w_in_p = jnp.concatenate(
        [wi[:, 0:3 * D_MODEL], wi[:, o_ga:o_q], wi[:, o_q:o_kb], wi[:, o_gb:o_ma], wi[:, o_ma:o_mb],
         wi[:, o_mb:o_mb + D_MODEL], wi[:, o_kb:o_vb], wi[:, o_vb:o_gb], wi[:, 3 * D_MODEL:a_cols]],
        axis=1).astype(BF16)
    conv_rkv = conv_a[l][:, 0:3 * D_MODEL]
    conv_wa = conv_a[l][:, 3 * D_MODEL:a_cols]
    pch = jnp.concatenate(
        [k_k[l][None], k_a[l][None], r_k[l].reshape(1, D_MODEL), ln_x_w[l][None], ln_x_b[l][None],
         0.5 * w0[l], 0.5 * a0[l], jnp.zeros((7, D_MODEL), F32)], axis=0)
    wup = w_up[l].reshape(2, LORA, N_PAIRS, LANES).transpose(0, 2, 1, 3)
    aup = a_up[l].reshape(2, LORA, N_PAIRS, LANES).transpose(0, 2, 1, 3)
    zl = jnp.zeros_like(wup)
    lora = jnp.concatenate([jnp.concatenate([wup, zl], axis=-1),
                            jnp.concatenate([zl, aup], axis=-1)], axis=-2)
    lora = (0.5 * lora).astype(BF16)
    woa, wob, wout = w_oA[l].astype(BF16), w_oB[l].astype(BF16), w_out[l].astype(BF16)
    sink_l = sink[l]

    cond8 = jnp.concatenate([c_ctx[None], c, jnp.zeros((8 - 1 - bd, D_MODEL), F32)], axis=0)
    mod3 = _modulation(cond8, w_ada[l], b_ada[l]).reshape(8, 3, D_MODEL)

    tm = 512
    ctx_row = lambda i: 0
    lat_row = lambda i: 1 + (i * tm) // td

    xp2 = x_prompt.reshape(bc * tc, D_MODEL)
    zc = _inproj(xp2, mod3, norm_w[l], w_in_p, ctx_row, tm)
    zc3 = zc.reshape(bc, tc, ZW)
    ya_c, st_c = _rwkv(zc3, conv_rkv, conv_wa, pch, lora, None, PAIRS_CTX)
    yb_c, new_k, new_v = _ctx_attention(sink_l, zc3)
    y_prompt = _merge(xp2, mod3, ya_c.reshape(bc * tc, D_MODEL), yb_c.reshape(bc * tc, D_MODEL), zc,
                      woa, wob, wout, final_norm_w, ctx_row, tm).reshape(bc, tc, D_MODEL)
    new_state = st_c[:, None]

    xs2 = x_sample.reshape(bd * td, D_MODEL)
    zd = _inproj(xs2, mod3, norm_w[l], w_in_p, lat_row, tm)
    zd3 = zd.reshape(bd, td, ZW)
    s0p = state_rwkv[:, l].reshape(bd, 2, N_PAIRS, 2, HEAD_DIM, HEAD_DIM).transpose(0, 1, 2, 4, 3, 5)
    s0p = s0p.reshape(bd, 2, N_PAIRS, HEAD_DIM, LANES)
    ya_d, _ = _rwkv(zd3, conv_rkv, conv_wa, pch, lora, s0p, PAIRS_LAT)
    cos_t, sin_t = _rope_tables(td)
    yb_d = _lat_attention(sink_l, zd3, _pad_heads(cache_k[:, l]), _pad_heads(cache_v[:, l], with_ones=True), cos_t, sin_t)
    y_sample = _merge(xs2, mod3, ya_d.reshape(bd * td, D_MODEL), yb_d.reshape(bd * td, D_MODEL), zd,
                      woa, wob, wout, final_norm_w, lat_row, tm).reshape(bd, td, D_MODEL)

    return (y_prompt, y_sample, new_k[:, None], new_v[:, None], new_state)
```

```python
import functools
import math

import jax
import jax.numpy as jnp
from jax import lax
from jax.experimental import pallas as pl
from jax.experimental.pallas import tpu as pltpu

F32 = jnp.float32
BF16 = jnp.bfloat16

D_MODEL = 1024
HEAD_DIM = 64
LANES = 128
N_PAIRS = D_MODEL // LANES
KV_HEADS = 4
KV_WIDTH = KV_HEADS * HEAD_DIM
LORA = 64
GRID_W = 64
WINDOW = 128
QBLK = 128
ROPE_BASE = 10000.0
RMS_EPS = 1e-6
GN_EPS = 64e-5
NEG_INF = -1e30
ATTN_SCALE = HEAD_DIM ** -0.5
PAIRS_CTX, PAIRS_LAT = 8, 2
GROUP = 4
HALF_DECAY_SCALE = -0.5 * math.exp(-0.5)
CHUNK = 64

COL_R, COL_K, COL_V, COL_GA, COL_Q, COL_GB, COL_MA, COL_MB = (i * D_MODEL for i in range(8))
COL_KB = 8 * D_MODEL
COL_VB = COL_KB + KV_WIDTH
COL_WA = COL_VB + KV_WIDTH
ZW = COL_WA + 2 * LORA
TN_IN = 2944

VMEM_LIMIT = 56 * 1024 * 1024


def _cparams(*sem):
    return pltpu.CompilerParams(dimension_semantics=sem, vmem_limit_bytes=VMEM_LIMIT)


def _sigmoid(x):
    return 0.5 * jnp.tanh(0.5 * x) + 0.5


def _mm(a, b):
    return jnp.dot(a.astype(BF16), b.astype(BF16), preferred_element_type=F32)


def _mm_nt(a, b):
    return lax.dot_general(a.astype(BF16), b.astype(BF16), (((1,), (1,)), ((), ())),
                           preferred_element_type=F32)


def _mod_kernel(c_ref, w_ref, b_ref, o_ref):
    c = c_ref[...]
    o_ref[...] = jnp.dot(c * _sigmoid(c), w_ref[...], preferred_element_type=F32) + b_ref[...]


def _modulation(cond8, w_ada, b_ada):
    n = w_ada.shape[1]
    tn = D_MODEL
    return pl.pallas_call(
        _mod_kernel,
        grid=(n // tn,),
        in_specs=[pl.BlockSpec((8, D_MODEL), lambda j: (0, 0)),
                  pl.BlockSpec((D_MODEL, tn), lambda j: (0, j)),
                  pl.BlockSpec((1, tn), lambda j: (0, j))],
        out_specs=pl.BlockSpec((8, tn), lambda j: (0, j)),
        out_shape=jax.ShapeDtypeStruct((8, n), F32),
        compiler_params=_cparams("arbitrary"),
        name="mod",
    )(cond8, w_ada, b_ada.reshape(1, n))


def _inproj_kernel(x_ref, mod_ref, nw_ref, w_ref, o_ref):
    x = x_ref[...]
    y = x * lax.rsqrt(jnp.mean(x * x, axis=-1, keepdims=True) + RMS_EPS) * nw_ref[...]
    h = y * (1.0 + mod_ref[0, 1:2, :]) + mod_ref[0, 0:1, :]
    o_ref[...] = jnp.dot(h.astype(BF16), w_ref[...], preferred_element_type=F32)


def _inproj(x2, mod3, norm_w, w_in_p, mod_row, tm):
    m = x2.shape[0]
    return pl.pallas_call(
        _inproj_kernel,
        grid=(ZW // TN_IN, m // tm),
        in_specs=[pl.BlockSpec((tm, D_MODEL), lambda n, i: (i, 0)),
                  pl.BlockSpec((1, 3, D_MODEL), lambda n, i: (mod_row(i), 0, 0)),
                  pl.BlockSpec((1, D_MODEL), lambda n, i: (0, 0)),
                  pl.BlockSpec((D_MODEL, TN_IN), lambda n, i: (0, n))],
        out_specs=pl.BlockSpec((tm, TN_IN), lambda n, i: (i, n)),
        out_shape=jax.ShapeDtypeStruct((m, ZW), F32),
        compiler_params=_cparams("arbitrary", "arbitrary"),
        name="inproj",
    )(x2, mod3, norm_w.reshape(1, D_MODEL), w_in_p)


def _segsum(x, e_bd):
    return jnp.dot(x.astype(BF16), e_bd, preferred_element_type=F32)


def _bmm(a, b):
    return jnp.einsum("bij,bjk->bik", a.astype(BF16), b.astype(BF16), preferred_element_type=F32)


def _bmm_nt(a, b):
    return jnp.einsum("bik,bjk->bij", a.astype(BF16), b.astype(BF16), preferred_element_type=F32)


def _bmm_tn(a, b):
    return jnp.einsum("bki,bkj->bij", a.astype(BF16), b.astype(BF16), preferred_element_type=F32)


def _chunk_local(r, kd, v, a, b, lw):
    c = CHUNK
    n = r.shape[0]
    g = n // 2

    def by_dir(fwd, bwd):
        return jnp.concatenate([fwd, bwd], axis=0)

    def masked(mask_f, mask_r, x):
        return by_dir(jnp.where(mask_f, x[:g], 0.0), jnp.where(mask_r, x[g:], 0.0))

    ti = lax.broadcasted_iota(jnp.int32, (c, c), 0)
    tj = lax.broadcasted_iota(jnp.int32, (c, c), 1)
    tri = by_dir(jnp.broadcast_to(jnp.where(tj <= ti, 1.0, 0.0).astype(BF16), (g, c, c)),
                 jnp.broadcast_to(jnp.where(tj >= ti, 1.0, 0.0).astype(BF16), (g, c, c)))
    lw_hi = lw.astype(BF16)
    lw_lo = (lw - lw_hi.astype(F32)).astype(BF16)
    cs = _bmm(tri, lw_hi) + _bmm(tri, lw_lo)
    tot = by_dir(cs[:g, c - 1:c], cs[g:, 0:1])
    e_in = jnp.exp(cs)
    e_neg = jnp.exp(-cs)
    pc = jnp.exp(tot)
    e_end = pc * e_neg
    at = a * jnp.exp(cs - lw)
    rt = r * e_in
    bt = b * e_neg
    kt = kd * e_neg
    bh = b * e_end
    kh = kd * e_end

    h0 = lax.broadcasted_iota(jnp.int32, (c, LANES), 1) < HEAD_DIM

    def stack2(x):
        m = h0 if x.shape[2] == LANES else jnp.concatenate([h0, h0], axis=1)
        xb = x.astype(BF16)
        zero = jnp.zeros_like(xb)
        return jnp.concatenate([jnp.where(m, xb, zero), jnp.where(m, zero, xb)], axis=1)

    def mul(x, y):
        return _bmm(x, stack2(y))

    gm = _bmm_nt(jnp.concatenate([at, rt], axis=1), jnp.concatenate([stack2(bt), stack2(kt)], axis=1))

    ri = lax.broadcasted_iota(jnp.int32, (c, LANES), 0)
    ci = lax.broadcasted_iota(jnp.int32, (c, LANES), 1) & (c - 1)
    diag = ri == ci

    a_ab = masked(ci < ri, ci > ri, gm[:, 0:c, 0:LANES])
    a_ak = masked(ci < ri, ci > ri, gm[:, 0:c, LANES:])
    a_rb = masked(ci <= ri, ci >= ri, gm[:, c:, 0:LANES])
    a_rk = masked(ci <= ri, ci >= ri, gm[:, c:, LANES:])

    def blk(shift):
        return (ri >> shift) == (ci >> shift)

    eye = jnp.where(diag, 1.0, 0.0)
    a8 = jnp.where(blk(3), a_ab, 0.0)
    a8_2 = mul(a8, a8)
    a8_4 = mul(a8_2, a8_2)
    tm = mul(mul(eye + a8, eye + a8_2), eye + a8_4)
    for sh in (3, 4, 5):
        off = jnp.where(blk(sh + 1) & jnp.logical_not(blk(sh)), a_ab, 0.0)
        tm2 = stack2(tm)
        tm = tm + _bmm(mul(tm, off), tm2)

    v2 = stack2(v)
    wu = _bmm(tm, jnp.concatenate([stack2(at), stack2(mul(a_ak, v))], axis=2))
    wu2 = stack2(wu)
    qy = _bmm(a_rb, wu2)
    qeff = rt + qy[:, :, :LANES]
    yloc = qy[:, :, LANES:] + _bmm(a_rk, v2)
    lhs_t = jnp.concatenate([jnp.concatenate([bh, pltpu.roll(bh, HEAD_DIM, axis=2)], axis=1),
                             jnp.concatenate([kh, pltpu.roll(kh, HEAD_DIM, axis=2)], axis=1)], axis=1)
    rhs = jnp.concatenate([wu2, jnp.concatenate([jnp.zeros_like(v2), v2], axis=2)], axis=1)
    md = _bmm_tn(lhs_t, rhs)[:, :HEAD_DIM]
    mt = md[:, :, :LANES] + jnp.where(diag, pc, 0.0)
    dt = md[:, :, LANES:]
    return qeff, yloc, mt, dt


def _rwkv_kernel(*refs, has_s0, pw):
    if has_s0:
        (r_ref, k_ref, v_ref, wa_ref, cr_ref, ck_ref, cv_ref, cwa_ref, pch_ref, lora_ref, s0_ref,
         ya_ref, so_ref, r_s, v_s, a_s, bon_s, kd_s, b_s, lw_s, y_s, q_s, mt_s, dt_s, z_s) = refs
    else:
        (r_ref, k_ref, v_ref, wa_ref, cr_ref, ck_ref, cv_ref, cwa_ref, pch_ref, lora_ref,
         ya_ref, so_ref, r_s, v_s, a_s, bon_s, kd_s, b_s, lw_s, y_s, q_s, mt_s, dt_s, z_s) = refs
        s0_ref = None
    t = r_ref.shape[1]
    nc = t // CHUNK
    h0 = lax.broadcasted_iota(jnp.int32, (t, LANES), 1) < HEAD_DIM
    ri = lax.broadcasted_iota(jnp.int32, (LANES, LANES), 0)
    ci = lax.broadcasted_iota(jnp.int32, (LANES, LANES), 1)
    e_bd = jnp.where((ri >> 6) == (ci >> 6), 1.0, 0.0).astype(BF16)

    row8 = lax.broadcasted_iota(jnp.int32, (8, LANES), 0)

    def conv(x, cw):
        prev = pltpu.roll(x, 1, axis=0)
        nxt = pltpu.roll(x, t - 1, axis=0)

        def taps(p, c, n):
            return cw[0:1, :] * p + cw[1:2, :] * c + cw[2:3, :] * n

        first = taps(jnp.where(row8 == 0, 0.0, prev[0:8]), x[0:8], nxt[0:8])
        last = taps(prev[t - 8:t], x[t - 8:t], jnp.where(row8 == 7, 0.0, nxt[t - 8:t]))
        mid = taps(prev[8:t - 8], x[8:t - 8], nxt[8:t - 8])
        return jnp.concatenate([first, mid, last], axis=0)

    wa = conv(wa_ref[0], cwa_ref[...])
    lora_in = jnp.where(h0, jnp.tanh(wa), wa).astype(BF16)
    for p in range(pw):
        ls = slice(p * LANES, (p + 1) * LANES)
        pch = pch_ref[:, ls]
        r = conv(r_ref[0, :, ls], cr_ref[:, ls])
        k = conv(k_ref[0, :, ls], ck_ref[:, ls])
        v = conv(v_ref[0, :, ls], cv_ref[:, ls])
        k_k, k_a, r_k = pch[0:1], pch[1:2], pch[2:3]
        kk = k * k_k
        kk = kk * lax.rsqrt(jnp.maximum(_segsum(kk * kk, e_bd), 1e-12))
        r_s[p] = r
        v_s[p] = v
        a_s[p] = -kk
        bon_s[p] = _segsum(r * k * r_k, e_bd) * v
        half_kk = 0.5 * kk
        for d in range(2):
            lo = jnp.dot(lora_in, lora_ref[d, p], preferred_element_type=F32)
            th_w = jnp.tanh(pch[5 + d:6 + d] + lo[:, :LANES])
            th_a = jnp.tanh(pch[7 + d:8 + d] + lo[:, LANES:])
            lw_s[p, d] = HALF_DECAY_SCALE * th_w + HALF_DECAY_SCALE
            kd_s[p, d] = k * ((1.0 - 0.5 * k_a) + (0.5 * k_a) * th_a)
            b_s[p, d] = half_kk * th_a + half_kk

    gsz = GROUP * CHUNK

    def local_body(g, carry):
        rows = pl.ds(pl.multiple_of(g * gsz, gsz), gsz)
        cidx = pl.ds(pl.multiple_of(g * GROUP, GROUP), GROUP)

        def shared(ref):
            x = jnp.concatenate([ref[p, rows, :].reshape(GROUP, CHUNK, LANES) for p in range(pw)], axis=0)
            return jnp.concatenate([x, x], axis=0)

        def per_dir(ref):
            return jnp.concatenate([ref[p, d, rows, :].reshape(GROUP, CHUNK, LANES)
                                    for d in range(2) for p in range(pw)], axis=0)

        qeff, yloc, mt, dt = _chunk_local(shared(r_s), per_dir(kd_s), shared(v_s), shared(a_s),
                                          per_dir(b_s), per_dir(lw_s))
        for d in range(2):
            for p in range(pw):
                part = slice((d * pw + p) * GROUP, (d * pw + p + 1) * GROUP)
                q_s[p, d, rows, :] = qeff[part].reshape(gsz, LANES)
                y_s[p, d, rows, :] = yloc[part].reshape(gsz, LANES)
                mt_s[p, d, cidx] = mt[part]
                dt_s[p, d, cidx] = dt[part]
        return carry

    lax.fori_loop(0, nc // GROUP, local_body, 0)

    hl = lax.broadcasted_iota(jnp.int32, (HEAD_DIM, LANES), 1) < HEAD_DIM

    def stack2(x):
        return jnp.concatenate([jnp.where(hl, x, 0.0), jnp.where(hl, 0.0, x)], axis=0)

    for p in range(pw):
        for d in range(2):
            if s0_ref is None:
                z_s[p, d] = dt_s[p, d, 0 if d == 0 else nc - 1]
            else:
                zt = stack2(s0_ref[0, d, p]).T
                z_s[p, d] = zt[:HEAD_DIM] + zt[HEAD_DIM:]

    def seq_body(i, carry):
        for p in range(pw):
            for d in range(2):
                c = i if d == 0 else nc - 1 - i
                sl = pl.ds(pl.multiple_of(c * CHUNK, CHUNK), CHUNK)
                zb = stack2(z_s[p, d]).astype(BF16)
                y_s[p, d, sl, :] = y_s[p, d, sl, :] + jnp.dot(q_s[p, d, sl, :].astype(BF16), zb,
                                                              preferred_element_type=F32)
                z_s[p, d] = jnp.dot(mt_s[p, d, c].astype(BF16), zb, preferred_element_type=F32) + dt_s[p, d, c]
        return carry

    lax.fori_loop(1 if s0_ref is None else 0, nc, seq_body, 0)

    for p in range(pw):
        ls = slice(p * LANES, (p + 1) * LANES)
        for d in range(2):
            zt = stack2(z_s[p, d]).T
            so_ref[0, d, 2 * p] = zt[:HEAD_DIM, :HEAD_DIM]
            so_ref[0, d, 2 * p + 1] = zt[HEAD_DIM:, HEAD_DIM:]
        y = y_s[p, 0] + y_s[p, 1]
        mu = _segsum(y, e_bd) * (1.0 / HEAD_DIM)
        yc = y - mu
        var = _segsum(yc * yc, e_bd) * (1.0 / HEAD_DIM)
        ya_ref[0, :, ls] = (yc * lax.rsqrt(var + GN_EPS) * pch_ref[3:4, ls] + pch_ref[4:5, ls] + bon_s[p])


def _rwkv(z3, conv_rkv, conv_wa, pch, lora, s0p, pw):
    bsz, t, _ = z3.shape
    nc = t // CHUNK
    wl = pw * LANES
    has_s0 = s0p is not None

    def zspec(col):
        return pl.BlockSpec((1, t, wl), lambda b, j, col=col: (b, 0, col // wl + j))

    def cspec(col):
        return pl.BlockSpec((3, wl), lambda b, j, col=col: (0, col // wl + j))

    in_specs = [zspec(COL_R), zspec(COL_K), zspec(COL_V),
                pl.BlockSpec((1, t, LANES), lambda b, j: (b, 0, COL_WA // LANES)),
                cspec(COL_R), cspec(COL_K), cspec(COL_V),
                pl.BlockSpec((3, LANES), lambda b, j: (0, 0)),
                pl.BlockSpec((16, wl), lambda b, j: (0, j)),
                pl.BlockSpec((2, pw, LANES, 2 * LANES), lambda b, j: (0, j, 0, 0))]
    args = [z3, z3, z3, z3, conv_rkv, conv_rkv, conv_rkv, conv_wa, pch, lora]
    if has_s0:
        in_specs.append(pl.BlockSpec((1, 2, pw, HEAD_DIM, LANES), lambda b, j: (b, 0, j, 0, 0)))
        args.append(s0p)
    tl = (t, LANES)
    scratch = [pltpu.VMEM((pw,) + tl, F32)] * 4 + [pltpu.VMEM((pw, 2) + tl, F32)] * 5 \
        + [pltpu.VMEM((pw, 2, nc, HEAD_DIM, LANES), F32)] * 2 + [pltpu.VMEM((pw, 2, HEAD_DIM, LANES), F32)]
    return pl.pallas_call(
        functools.partial(_rwkv_kernel, has_s0=has_s0, pw=pw),
        grid=(bsz, N_PAIRS // pw),
        in_specs=in_specs,
        out_specs=[pl.BlockSpec((1, t, wl), lambda b, j: (b, 0, j)),
                   pl.BlockSpec((1, 2, 2 * pw, HEAD_DIM, HEAD_DIM), lambda b, j: (b, 0, j, 0, 0))],
        out_shape=[jax.ShapeDtypeStruct((bsz, t, D_MODEL), F32),
                   jax.ShapeDtypeStruct((bsz, 2, 2 * N_PAIRS, HEAD_DIM, HEAD_DIM), F32)],
        scratch_shapes=scratch,
        compiler_params=_cparams("arbitrary", "arbitrary"),
        name="rwkv_s0" if has_s0 else "rwkv",
    )(*args)


def _split_heads(x, odd):
    lo = lax.broadcasted_iota(jnp.int32, x.shape, 1) < HEAD_DIM
    if odd:
        xb = jnp.where(lo, 0.0, x)
        return pltpu.roll(xb, HEAD_DIM, axis=1), xb
    xa = jnp.where(lo, x, 0.0)
    return xa, pltpu.roll(xa, HEAD_DIM, axis=1)


def _softmax_parts(parts, sink):
    tiles = [p[:, j * LANES:(j + 1) * LANES] for p in parts for j in range(p.shape[1] // LANES)]
    m = jnp.maximum(jnp.max(functools.reduce(jnp.maximum, tiles), axis=-1, keepdims=True), sink)
    return [jnp.exp(p - m) for p in parts], jnp.exp(sink - m)


def _with_ones(vab):
    r2 = vab.shape[0]
    lo = lax.broadcasted_iota(jnp.int32, (r2, LANES), 1) < HEAD_DIM
    top = lax.broadcasted_iota(jnp.int32, (r2, LANES), 0) < r2 // 2
    ones = jnp.where(lo == top, 1.0, 0.0).astype(vab.dtype)
    return jnp.concatenate([vab, ones], axis=1)


def _ctx_attn_kernel(sink_ref, q_ref, k_ref, v_ref, o_ref, nk_ref, nv_ref):
    t = q_ref.shape[1]
    k = k_ref[0]
    v = v_ref[0]
    lane_lo = lax.broadcasted_iota(jnp.int32, (t, LANES), 1) < HEAD_DIM
    for hk in range(KV_HEADS):
        nk_ref[0, hk] = k[:, hk * HEAD_DIM:(hk + 1) * HEAD_DIM]
        nv_ref[0, hk] = v[:, hk * HEAD_DIM:(hk + 1) * HEAD_DIM]
    for hk in range(KV_HEADS):
        cb = hk // 2
        ka, kb = _split_heads(k[:, cb * LANES:(cb + 1) * LANES], hk % 2)
        va, vb = _split_heads(v[:, cb * LANES:(cb + 1) * LANES], hk % 2)
        kab = jnp.concatenate([ka, kb], axis=0).astype(BF16)
        vab = _with_ones(jnp.concatenate([va, vb], axis=0).astype(BF16))
        for pp in range(2):
            p = 2 * hk + pp
            q2 = (q_ref[0, :, p * LANES:(p + 1) * LANES] * ATTN_SCALE).astype(BF16)
            s = _mm_nt(q2, kab)
            e0, x0 = _softmax_parts([s[:, :t]], sink_ref[hk, 2 * pp])
            e1, x1 = _softmax_parts([s[:, t:]], sink_ref[hk, 2 * pp + 1])
            o = _mm(jnp.concatenate([e0[0], e1[0]], axis=1), vab)
            o_ref[0, :, p * LANES:(p + 1) * LANES] = o[:, :LANES] / (o[:, LANES:] + jnp.where(lane_lo, x0, x1))


def _ctx_attention(sink, z3):
    bsz, t, _ = z3.shape
    return pl.pallas_call(
        _ctx_attn_kernel,
        grid=(bsz,),
        in_specs=[pl.BlockSpec(memory_space=pltpu.SMEM),
                  pl.BlockSpec((1, t, D_MODEL), lambda b: (b, 0, COL_Q // D_MODEL)),
                  pl.BlockSpec((1, t, KV_WIDTH), lambda b: (b, 0, COL_KB // KV_WIDTH)),
                  pl.BlockSpec((1, t, KV_WIDTH), lambda b: (b, 0, COL_VB // KV_WIDTH))],
        out_specs=[pl.BlockSpec((1, t, D_MODEL), lambda b: (b, 0, 0)),
                   pl.BlockSpec((1, KV_HEADS, t, HEAD_DIM), lambda b: (b, 0, 0, 0)),
                   pl.BlockSpec((1, KV_HEADS, t, HEAD_DIM), lambda b: (b, 0, 0, 0))],
        out_shape=[jax.ShapeDtypeStruct((bsz, t, D_MODEL), F32),
                   jax.ShapeDtypeStruct((bsz, KV_HEADS, t, HEAD_DIM), F32),
                   jax.ShapeDtypeStruct((bsz, KV_HEADS, t, HEAD_DIM), F32)],
        compiler_params=_cparams("arbitrary"),
        name="ctx_attn",
    )(sink, z3, z3, z3)


def _rope(x, cos, sin_signed):
    lane = lax.broadcasted_iota(jnp.int32, x.shape, 1)
    partner = jnp.where((lane & 31) < 16, pltpu.roll(x, LANES - 16, axis=1), pltpu.roll(x, 16, axis=1))
    return x * cos + partner * sin_signed


def _lat_attn_kernel(sink_ref, q_ref, k_ref, v_ref, ck_ref, cv_ref, cos_ref, sin_ref, o_ref, kab_s, vab_s):
    t = k_ref.shape[1]
    nb = t // QBLK
    i = pl.program_id(1)

    @pl.when(i == 0)
    def _():
        cos = cos_ref[...]
        sin = sin_ref[...]
        for cb in range(KV_HEADS // 2):
            kr = _rope(k_ref[0, :, cb * LANES:(cb + 1) * LANES], cos, sin)
            vv = v_ref[0, :, cb * LANES:(cb + 1) * LANES]
            for odd in range(2):
                hk = 2 * cb + odd
                ka, kb = _split_heads(kr, odd)
                va, vb = _split_heads(vv, odd)
                for j in range(nb):
                    rows = slice(j * QBLK, (j + 1) * QBLK)
                    kab_s[hk, j, 0:QBLK, :] = ka[rows].astype(BF16)
                    kab_s[hk, j, QBLK:2 * QBLK, :] = kb[rows].astype(BF16)
                    vab_s[hk, j] = _with_ones(jnp.concatenate([va[rows], vb[rows]], axis=0).astype(BF16))

    qrows = pl.ds(pl.multiple_of(i * QBLK, QBLK), QBLK)
    cos_q = cos_ref[qrows, :]
    sin_q = sin_ref[qrows, :]
    ri = lax.broadcasted_iota(jnp.int32, (QBLK, QBLK), 0)
    ci = lax.broadcasted_iota(jnp.int32, (QBLK, QBLK), 1)
    mask_prev = ci >= ri + jnp.where(i >= 1, 0, QBLK)
    mask_next = ci <= ri - jnp.where(i + 1 < nb, 0, QBLK)
    lane_lo = lax.broadcasted_iota(jnp.int32, (QBLK, LANES), 1) < HEAD_DIM
    jm = jnp.maximum(i - 1, 0)
    jp = jnp.minimum(i + 1, nb - 1)
    for hk in range(KV_HEADS):
        kblk = [kab_s[hk, jm], kab_s[hk, i], kab_s[hk, jp]]
        vblk = [vab_s[hk, jm], vab_s[hk, i], vab_s[hk, jp]]
        ckh = ck_ref[0, hk]
        cvh = cv_ref[0, hk]
        tc = ckh.shape[0] // 2
        for pp in range(2):
            p = 2 * hk + pp
            q2 = (_rope(q_ref[0, :, p * LANES:(p + 1) * LANES], cos_q, sin_q) * ATTN_SCALE).astype(BF16)
            s_loc = [_mm_nt(q2, kb) for kb in kblk]
            s_ctx = _mm_nt(q2, ckh)
            exps, invs = [], []
            for half in range(2):
                cols = slice(half * QBLK, (half + 1) * QBLK)
                parts = [jnp.where(mask_prev, s_loc[0][:, cols], NEG_INF),
                         s_loc[1][:, cols],
                         jnp.where(mask_next, s_loc[2][:, cols], NEG_INF),
                         s_ctx[:, half * tc:(half + 1) * tc]]
                es, sink_e = _softmax_parts(parts, sink_ref[hk, 2 * pp + half])
                exps.append(es)
                invs.append(sink_e)
            o = _mm(jnp.concatenate([exps[0][3], exps[1][3]], axis=1), cvh)
            for jj in range(3):
                o = o + _mm(jnp.concatenate([exps[0][jj], exps[1][jj]], axis=1), vblk[jj])
            o_ref[0, :, p * LANES:(p + 1) * LANES] = o[:, :LANES] / (o[:, LANES:] + jnp.where(lane_lo, invs[0], invs[1]))


def _lat_attention(sink, z3, ck2, cv2, cos_t, sin_t):
    bsz, t, _ = z3.shape
    nb = t // QBLK
    tc2 = ck2.shape[2]
    return pl.pallas_call(
        _lat_attn_kernel,
        grid=(bsz, nb),
        in_specs=[pl.BlockSpec(memory_space=pltpu.SMEM),
                  pl.BlockSpec((1, QBLK, D_MODEL), lambda b, i: (b, i, COL_Q // D_MODEL)),
                  pl.BlockSpec((1, t, KV_WIDTH), lambda b, i: (b, 0, COL_KB // KV_WIDTH)),
                  pl.BlockSpec((1, t, KV_WIDTH), lambda b, i: (b, 0, COL_VB // KV_WIDTH)),
                  pl.BlockSpec((1, KV_HEADS, tc2, LANES), lambda b, i: (b, 0, 0, 0)),
                  pl.BlockSpec((1, KV_HEADS, tc2, 2 * LANES), lambda b, i: (b, 0, 0, 0)),
                  pl.BlockSpec((t, LANES), lambda b, i: (0, 0)),
                  pl.BlockSpec((t, LANES), lambda b, i: (0, 0))],
        out_specs=pl.BlockSpec((1, QBLK, D_MODEL), lambda b, i: (b, i, 0)),
        out_shape=jax.ShapeDtypeStruct((bsz, t, D_MODEL), F32),
        scratch_shapes=[pltpu.VMEM((KV_HEADS, nb, 2 * QBLK, LANES), BF16),
                        pltpu.VMEM((KV_HEADS, nb, 2 * QBLK, 2 * LANES), BF16)],
        compiler_params=_cparams("arbitrary", "arbitrary"),
        name="lat_attn",
    )(sink, z3, z3, z3, ck2, cv2, cos_t, sin_t)


def _merge_kernel(x_ref, mod_ref, ya_ref, yb_ref, ga_ref, gb_ref, ma_ref, mb_ref,
                  woa_ref, wob_ref, wout_ref, fw_ref, o_ref):
    ga = ga_ref[...]
    gb = gb_ref[...]
    br_a = jnp.dot((ya_ref[...] * (ga * _sigmoid(ga))).astype(BF16), woa_ref[...], preferred_element_type=F32)
    br_b = jnp.dot((yb_ref[...] * (gb * _sigmoid(gb))).astype(BF16), wob_ref[...], preferred_element_type=F32)
    merged = _sigmoid(ma_ref[...]) * br_a + _sigmoid(mb_ref[...]) * br_b
    out = x_ref[...] + mod_ref[0, 2:3, :] * jnp.dot(merged.astype(BF16), wout_ref[...],
                                                     preferred_element_type=F32)
    o_ref[...] = out * lax.rsqrt(jnp.mean(out * out, axis=-1, keepdims=True) + RMS_EPS) * fw_ref[...]


def _merge(x2, mod3, ya2, yb2, z2, woa, wob, wout, fw, mod_row, tm):
    m = x2.shape[0]

    def tok(col=0):
        return pl.BlockSpec((tm, D_MODEL), lambda i, col=col: (i, col // D_MODEL))

    def wspec():
        return pl.BlockSpec((D_MODEL, D_MODEL), lambda i: (0, 0))

    return pl.pallas_call(
        _merge_kernel,
        grid=(m // tm,),
        in_specs=[tok(), pl.BlockSpec((1, 3, D_MODEL), lambda i: (mod_row(i), 0, 0)),
                  tok(), tok(), tok(COL_GA), tok(COL_GB), tok(COL_MA), tok(COL_MB),
                  wspec(), wspec(), wspec(), pl.BlockSpec((1, D_MODEL), lambda i: (0, 0))],
        out_specs=tok(),
        out_shape=jax.ShapeDtypeStruct((m, D_MODEL), F32),
        compiler_params=_cparams("arbitrary"),
        name="merge",
    )(x2, mod3, ya2, yb2, z2, z2, z2, z2, woa, wob, wout, fw.reshape(1, D_MODEL))


def _rope_tables(t):
    n_rows = t // GRID_W
    row = jnp.repeat(jnp.arange(n_rows), GRID_W)
    col = jnp.tile(jnp.arange(GRID_W), n_rows)
    nf = HEAD_DIM // 4
    inv = ROPE_BASE ** (-jnp.arange(nf, dtype=F32) / nf)
    ang_r = row.astype(F32)[:, None] * inv[None, :]
    ang_c = col.astype(F32)[:, None] * inv[None, :]
    cos = jnp.concatenate([jnp.cos(ang_r), jnp.cos(ang_r), jnp.cos(ang_c), jnp.cos(ang_c)], axis=-1)
    sin = jnp.concatenate([-jnp.sin(ang_r), jnp.sin(ang_r), -jnp.sin(ang_c), jnp.sin(ang_c)], axis=-1)
    return jnp.tile(cos, (1, 2)), jnp.tile(sin, (1, 2))


def _pad_heads(x, with_ones=False):
    zero = jnp.zeros_like(x)
    cols = [jnp.concatenate([x, zero], axis=-2), jnp.concatenate([zero, x], axis=-2)]
    if with_ones:
        one = jnp.ones_like(x)
        cols += [jnp.concatenate([one, zero], axis=-2), jnp.concatenate([zero, one], axis=-2)]
    return jnp.concatenate(cols, axis=-1).astype(BF16)


def kernel(x_prompt, x_sample, cache_k, cache_v, state_rwkv, c, c_ctx, norm_w, w_ada, b_ada, w_in, conv_a,
           w0, w_up, a0, a_up, k_k, k_a, r_k, ln_x_w, ln_x_b, w_oA, sink, w_oB, w_out, final_norm_w):
    depth = norm_w.shape[0]
    assert depth == 1
    bc, tc, _ = x_prompt.shape
    bd, td, _ = x_sample.shape
    l = 0
    a_cols = 3 * D_MODEL + 2 * LORA

    wi = w_in[l]
    o_ga = a_cols
    o_q = o_ga + D_MODEL
    o_kb = o_q + D_MODEL
    o_vb = o_kb + KV_WIDTH
    o_gb = o_vb + KV_WIDTH
    o_ma = o_gb + D_MODEL
    o_mb = o_ma + D_MODEL
    w_in_p = jnp.concatenate(
        [wi[:, 0:3 * D_MODEL], wi[:, o_ga:o_q], wi[:, o_q:o_kb], wi[:, o_gb:o_ma], wi[:, o_ma:o_mb],
         wi[:, o_mb:o_mb + D_MODEL], wi[:, o_kb:o_vb], wi[:, o_vb:o_gb], wi[:, 3 * D_MODEL:a_cols]],
        axis=1).astype(BF16)
    conv_rkv = conv_a[l][:, 0:3 * D_MODEL]
    conv_wa = conv_a[l][:, 3 * D_MODEL:a_cols]
    pch = jnp.concatenate(
        [k_k[l][None], k_a[l][None], r_k[l].reshape(1, D_MODEL), ln_x_w[l][None], ln_x_b[l][None],
         0.5 * w0[l], 0.5 * a0[l], jnp.zeros((7, D_MODEL), F32)], axis=0)
    wup = w_up[l].reshape(2, LORA, N_PAIRS, LANES).transpose(0, 2, 1, 3)
    aup = a_up[l].reshape(2, LORA, N_PAIRS, LANES).transpose(0, 2, 1, 3)
    zl = jnp.zeros_like(wup)
    lora = jnp.concatenate([jnp.concatenate([wup, zl], axis=-1),
                            jnp.concatenate([zl, aup], axis=-1)], axis=-2)
    lora = (0.5 * lora).astype(BF16)
    woa, wob, wout = w_oA[l].astype(BF16), w_oB[l].astype(BF16), w_out[l].astype(BF16)
    sink_l = sink[l]

    cond8 = jnp.concatenate([c_ctx[None], c, jnp.zeros((8 - 1 - bd, D_MODEL), F32)], axis=0)
    mod3 = _modulation(cond8, w_ada[l], b_ada[l]).reshape(8, 3, D_MODEL)

    tm = 512
    ctx_row = lambda i: 0
    lat_row = lambda i: 1 + (i * tm) // td

    xp2 = x_prompt.reshape(bc * tc, D_MODEL)
    zc = _inproj(xp2, mod3, norm_w[l], w_in_p, ctx_row, tm)
    zc3 = zc.reshape(bc, tc, ZW)
    ya_c, st_c = _rwkv(zc3, conv_rkv, conv_wa, pch, lora, None, PAIRS_CTX)
    yb_c, new_k, new_v = _ctx_attention(sink_l, zc3)
    y_prompt = _merge(xp2, mod3, ya_c.reshape(bc * tc, D_MODEL), yb_c.reshape(bc * tc, D_MODEL), zc,
                      woa, wob, wout, final_norm_w, ctx_row, tm).reshape(bc, tc, D_MODEL)
    new_state = st_c[:, None]

    xs2 = x_sample.reshape(bd * td, D_MODEL)
    zd = _inproj(xs2, mod3, norm_w[l], w_in_p, lat_row, tm)
    zd3 = zd.reshape(bd, td, ZW)
    s0p = state_rwkv[:, l].reshape(bd, 2, N_PAIRS, 2, HEAD_DIM, HEAD_DIM).transpose(0, 1, 2, 4, 3, 5)
    s0p = s0p.reshape(bd, 2, N_PAIRS, HEAD_DIM, LANES)
    ya_d, _ = _rwkv(zd3, conv_rkv, conv_wa, pch, lora, s0p, PAIRS_LAT)
    cos_t, sin_t = _rope_tables(td)
    yb_d = _lat_attention(sink_l, zd3, _pad_heads(cache_k[:, l]), _pad_heads(cache_v[:, l], with_ones=True), cos_t, sin_t)
    y_sample = _merge(xs2, mod3, ya_d.reshape(bd * td, D_MODEL), yb_d.reshape(bd * td, D_MODEL), zd,
                      woa, wob, wout, final_norm_w, lat_row, tm).reshape(bd, td, D_MODEL)

    return (y_prompt, y_sample, new_k[:, None], new_v[:, None], new_state)
```
